```python
import math
import jax
import jax.numpy as jnp
from jax import lax
import numpy as np

D_MODEL = 2048
BATCH = 2
SEQ = 4096
DEPTH = 4

HEAD_DIM = 128
N_MIX_HEADS = D_MODEL // HEAD_DIM
A_Q_HEADS = N_MIX_HEADS // 4
A_KV_HEADS = max(1, A_Q_HEADS // 2)
A_GROUP = A_Q_HEADS // A_KV_HEADS
A_RADIUS = 128
B_PAIRS = ((128, 1), (512, 4), (2048, 16))
B_N_GROUPS = len(B_PAIRS)
B_HEADS_PER_GROUP = (N_MIX_HEADS - A_Q_HEADS) // B_N_GROUPS
B_KV_HEADS = B_HEADS_PER_GROUP
AB_IN_WIDTH = HEAD_DIM * (A_Q_HEADS + 2 * A_KV_HEADS + B_N_GROUPS * B_HEADS_PER_GROUP + 2 * B_KV_HEADS)
AB_OUT_WIDTH = HEAD_DIM * (A_Q_HEADS + B_HEADS_PER_GROUP)
C_HEADS = D_MODEL // (2 * HEAD_DIM)
C_IN_WIDTH = 3 * D_MODEL
D_FF = (D_MODEL * 43 // 16 + 127) // 128 * 128
CONV_WIDTH = 3
QBLOCK = 128
EPS = 1e-6
NEG_INF = -1e30
N_EVEN = (DEPTH + 1) // 2
N_ODD = DEPTH // 2

kernel_name = "hybrid_local_dilated_diff_encoder"


def rms_norm(x, g):
    xf = x.astype(jnp.float32)
    y = xf * lax.rsqrt(jnp.mean(xf * xf, axis=-1, keepdims=True) + EPS)
    return (y * g.astype(jnp.float32)).astype(x.dtype)


def alibi_slopes(n):
    return 2.0 ** (-8.0 * jnp.arange(1, n + 1, dtype=jnp.float32) / n)


def banded_attention(q, k, v, radius, slopes, step, sink=None):
    z, length, hk, g, dh = q.shape
    nb = -(-length // radius)
    pad = nb * radius - length
    qb = jnp.pad(q, ((0, 0), (0, pad), (0, 0), (0, 0), (0, 0))).reshape(z, nb, radius, hk, g, dh)
    kv_pad = ((0, 0), (radius, radius + pad), (0, 0), (0, 0))

    def windows(t):
        tb = jnp.pad(t, kv_pad).reshape(z, nb + 2, radius, hk, dh)
        return jnp.concatenate([tb[:, :-2], tb[:, 1:-1], tb[:, 2:]], axis=2)

    kw, vw = windows(k), windows(v)
    s = jnp.einsum('znqhgd,znkhd->znhgqk', qb, kw).astype(jnp.float32) * (dh ** -0.5)
    qpos = jnp.arange(nb * radius).reshape(nb, radius)
    kpos = jnp.arange(nb)[:, None] * radius - radius + jnp.arange(3 * radius)[None, :]
    dist = jnp.abs(kpos[:, None, :] - qpos[:, :, None])
    valid = (dist <= radius) & (kpos >= 0)[:, None, :] & (kpos < length)[:, None, :]
    bias = -(slopes.astype(jnp.float32)[None, :, :, None, None]
             * (step * dist).astype(jnp.float32)[:, None, None])
    s = jnp.where(valid[:, None, None], s + bias, NEG_INF)
    m = s.max(axis=-1)
    if sink is not None:
        sk = sink.astype(jnp.float32)[None, None, :, :, None]
        m = jnp.maximum(m, sk)
    p = jnp.exp(s - m[..., None])
    den = p.sum(axis=-1)
    if sink is not None:
        den = den + jnp.exp(sk - m)
    o = jnp.einsum('znhgqk,znkhd->znqhgd', p.astype(v.dtype), vw).astype(jnp.float32)
    den_t = jnp.transpose(den, (0, 1, 4, 2, 3))
    lse_t = jnp.transpose(m + jnp.log(den), (0, 1, 4, 2, 3))
    o = (o / den_t[..., None]).reshape(z, nb * radius, hk, g, dh)[:, :length]
    lse = lse_t.reshape(z, nb * radius, hk, g)[:, :length]
    return o.astype(q.dtype), lse


def fold_stride(t, d):
    b, s = t.shape[0], t.shape[1]
    rest = t.shape[2:]
    t = jnp.moveaxis(t.reshape((b, s // d, d) + rest), 2, 1)
    return t.reshape((b * d, s // d) + rest)


def unfold_stride(t, d, b):
    l = t.shape[1]
    rest = t.shape[2:]
    t = jnp.moveaxis(t.reshape((b, d, l) + rest), 1, 2)
    return t.reshape((b, l * d) + rest)


def local_dilated_mixer(h, w_in, w_out, sink):
    b, s, _ = h.shape
    proj = h @ w_in
    sizes = [A_Q_HEADS * HEAD_DIM, A_KV_HEADS * HEAD_DIM, A_KV_HEADS * HEAD_DIM,
             B_N_GROUPS * B_HEADS_PER_GROUP * HEAD_DIM, B_KV_HEADS * HEAD_DIM, B_KV_HEADS * HEAD_DIM]
    idx = [sum(sizes[:i + 1]) for i in range(len(sizes) - 1)]
    qa, ka, va, qb, kb, vb = jnp.split(proj, idx, axis=-1)
    slopes = alibi_slopes(N_MIX_HEADS)

    qa = qa.reshape(b, s, A_KV_HEADS, A_GROUP, HEAD_DIM)
    ka = ka.reshape(b, s, A_KV_HEADS, HEAD_DIM)
    va = va.reshape(b, s, A_KV_HEADS, HEAD_DIM)
    oa, _ = banded_attention(qa, ka, va, A_RADIUS, slopes[:A_Q_HEADS].reshape(A_KV_HEADS, A_GROUP), 1,
                             sink=sink.reshape(A_KV_HEADS, A_GROUP))
    oa = oa.reshape(b, s, A_Q_HEADS * HEAD_DIM)

    qb = qb.reshape(b, s, B_N_GROUPS, B_HEADS_PER_GROUP, HEAD_DIM)
    kb = kb.reshape(b, s, B_KV_HEADS, HEAD_DIM)
    vb = vb.reshape(b, s, B_KV_HEADS, HEAD_DIM)
    outs, lses = [], []
    for gi, (window, dil) in enumerate(B_PAIRS):
        radius = window // (2 * dil)
        start = A_Q_HEADS + gi * B_HEADS_PER_GROUP
        sl = slopes[start:start + B_HEADS_PER_GROUP][:, None]
        o, lse = banded_attention(fold_stride(qb[:, :, gi], dil)[:, :, :, None, :],
                                  fold_stride(kb, dil), fold_stride(vb, dil), radius, sl, dil)
        outs.append(unfold_stride(o[:, :, :, 0], dil, b))
        lses.append(unfold_stride(lse[..., 0], dil, b))
    wts = jax.nn.softmax(jnp.stack(lses, axis=0), axis=0)
    ob = jnp.sum(wts[..., None] * jnp.stack(outs, axis=0).astype(jnp.float32), axis=0)
    ob = ob.astype(h.dtype).reshape(b, s, B_HEADS_PER_GROUP * HEAD_DIM)
    return jnp.concatenate([oa, ob], axis=-1) @ w_out


def diff_attention(h, w_in, w_out, lam_params, subln_g, lambda_init):
    b, s, _ = h.shape
    q, k, v = jnp.split(h @ w_in, 3, axis=-1)
    q = q.reshape(b, s, C_HEADS, 2, HEAD_DIM)
    k = k.reshape(b, s, C_HEADS, 2, HEAD_DIM)
    v = v.reshape(b, s, C_HEADS, 2 * HEAD_DIM)
    lp = lam_params.astype(jnp.float32)
    lam = jnp.exp(jnp.sum(lp[0] * lp[1])) - jnp.exp(jnp.sum(lp[2] * lp[3])) + lambda_init
    slopes = alibi_slopes(C_HEADS)
    nb = s // QBLOCK
    qblocks = jnp.moveaxis(q.reshape(b, nb, QBLOCK, C_HEADS, 2, HEAD_DIM), 1, 0)
    kpos = jnp.arange(s)
    scale = HEAD_DIM ** -0.5

    def block(args):
        qblk, start = args
        sc = jnp.einsum('bqhjd,bkhjd->bhjqk', qblk, k).astype(jnp.float32) * scale
        qpos = start + jnp.arange(QBLOCK)
        dist = jnp.abs(qpos[:, None] - kpos[None, :]).astype(jnp.float32)
        sc = sc - slopes[None, :, None, None, None] * dist
        a = jax.nn.softmax(sc, axis=-1)
        attn = a[:, :, 0] - lam * a[:, :, 1]
        return jnp.einsum('bhqk,bkhe->bqhe', attn.astype(v.dtype), v)

    o = lax.map(block, (qblocks, jnp.arange(nb) * QBLOCK))
    o = jnp.moveaxis(o, 0, 1).reshape(b, s, C_HEADS, 2 * HEAD_DIM)
    o = rms_norm(o, subln_g) * (1.0 - lambda_init)
    return o.reshape(b, s, D_MODEL) @ w_out


def conv_ffn(h, w_up, conv_w, conv_b, w_down):
    gate, up = jnp.split(h @ w_up, 2, axis=-1)
    gp = jnp.pad(gate, ((0, 0), (1, 1), (0, 0)))
    gate = gp[:, :-2] * conv_w[0] + gp[:, 1:-1] * conv_w[1] + gp[:, 2:] * conv_w[2] + conv_b
    return (jax.nn.gelu(gate, approximate=True) * up) @ w_down


def setup_inputs(seed: int = 0) -> dict:
    key = jax.random.key(seed)
    ks = jax.random.split(key, 16)

    def nrm(k, shape, scale):
        return jax.random.normal(k, shape, jnp.float32) * scale

    return {
        "x": nrm(ks[0], (BATCH, SEQ, D_MODEL), 1.0),
        "c": nrm(ks[1], (BATCH, D_MODEL), 1.0),
        "ada_w": nrm(ks[2], (DEPTH, D_MODEL, 6 * D_MODEL), D_MODEL ** -0.5),
        "ada_b": nrm(ks[3], (DEPTH, 6 * D_MODEL), 0.02),
        "norm_g": 1.0 + nrm(ks[4], (DEPTH, 4, D_MODEL), 0.02),
        "ab_w_in": nrm(ks[5], (N_EVEN, D_MODEL, AB_IN_WIDTH), D_MODEL ** -0.5),
        "ab_w_out": nrm(ks[6], (N_EVEN, AB_OUT_WIDTH, D_MODEL), AB_OUT_WIDTH ** -0.5),
        "a_sink": nrm(ks[7], (N_EVEN, A_Q_HEADS), 0.5),
        "c_w_in": nrm(ks[8], (N_ODD, D_MODEL, C_IN_WIDTH), D_MODEL ** -0.5),
        "c_w_out": nrm(ks[9], (N_ODD, D_MODEL, D_MODEL), D_MODEL ** -0.5),
        "c_lambda": nrm(ks[10], (N_ODD, 4, HEAD_DIM), 0.1),
        "c_subln_g": 1.0 + nrm(ks[11], (N_ODD, 2 * HEAD_DIM), 0.02),
        "ffn_w_up": nrm(ks[12], (DEPTH, D_MODEL, 2 * D_FF), D_MODEL ** -0.5),
        "ffn_conv_w": nrm(ks[13], (DEPTH, CONV_WIDTH, D_FF), CONV_WIDTH ** -0.5),
        "ffn_conv_b": nrm(ks[14], (DEPTH, D_FF), 0.02),
        "ffn_w_down": nrm(ks[15], (DEPTH, D_FF, D_MODEL), D_FF ** -0.5),
    }


def reference(x, c, ada_w, ada_b, norm_g, ab_w_in, ab_w_out, a_sink, c_w_in, c_w_out,
              c_lambda, c_subln_g, ffn_w_up, ffn_conv_w, ffn_conv_b, ffn_w_down):
    cond = jax.nn.silu(c)
    for layer in range(DEPTH):
        mod = cond @ ada_w[layer] + ada_b[layer]
        sh1, sc1, g1, sh2, sc2, g2 = [m[:, None, :] for m in jnp.split(mod, 6, axis=-1)]
        h = rms_norm(x, norm_g[layer, 0]) * (1.0 + sc1) + sh1
        j = layer // 2
        if layer % 2 == 0:
            y = local_dilated_mixer(h, ab_w_in[j], ab_w_out[j], a_sink[j])
        else:
            lambda_init = 0.8 - 0.6 * math.exp(-0.3 * layer)
            y = diff_attention(h, c_w_in[j], c_w_out[j], c_lambda[j], c_subln_g[j], lambda_init)
        x = x + g1 * rms_norm(y, norm_g[layer, 1])
        h = rms_norm(x, norm_g[layer, 2]) * (1.0 + sc2) + sh2
        y = conv_ffn(h, ffn_w_up[layer], ffn_conv_w[layer], ffn_conv_b[layer], ffn_w_down[layer])
        x = x + g2 * rms_norm(y, norm_g[layer, 3])
    return x
```

```python
import functools
import math

import numpy as np
import jax
import jax.numpy as jnp
from jax import lax
from jax.experimental import pallas as pl
from jax.experimental.pallas import tpu as pltpu

F32 = jnp.float32
BF16 = jnp.bfloat16

HEAD_DIM = 128
A_Q_HEADS = 4
A_KV_HEADS = 2
A_GROUP = A_Q_HEADS // A_KV_HEADS
A_RADIUS = 128
B_PAIRS = ((128, 1), (512, 4), (2048, 16))
B_HEADS = 4
N_MIX_HEADS = A_Q_HEADS + len(B_PAIRS) * B_HEADS
C_HEADS = 8
CONV_WIDTH = 3
EPS = 1e-6
NEG_INF = -1e30

SUBLANES_F32 = 8
SUBLANES_BF16 = 16
LANES = 128
VMEM_LIMIT_BYTES = 56 * 1024 * 1024

AB_BLOCK_Q = 128
FFN_HALO = SUBLANES_BF16


def _alibi_slopes(n):
    return (2.0 ** (-8.0 * np.arange(1, n + 1, dtype=np.float32) / n)).astype(np.float32)


def _rms(x):
    return x * lax.rsqrt(jnp.mean(x * x, axis=-1, keepdims=True) + EPS)


def _dot(a, b):
    return jnp.dot(a, b, preferred_element_type=F32)


def _dot_nt(a, b):
    return lax.dot_general(a, b, (((1,), (1,)), ((), ())), preferred_element_type=F32)


def _ada_kernel(c_ref, w_ref, b_ref, o_ref):
    c = c_ref[...]
    cond = c * jax.nn.sigmoid(c)
    o_ref[0] = jnp.dot(cond, w_ref[0], preferred_element_type=F32,
                       precision=lax.Precision.HIGHEST) + b_ref[0]


def _ada(c, ada_w, ada_b, tn=1024):
    depth, d, n = ada_w.shape
    b = c.shape[0]
    rows = -(-b // SUBLANES_F32) * SUBLANES_F32
    c_pad = jnp.zeros((rows, d), F32).at[:b].set(c)
    out = pl.pallas_call(
        _ada_kernel,
        grid=(depth, n // tn),
        in_specs=[
            pl.BlockSpec((rows, d), lambda l, j: (0, 0)),
            pl.BlockSpec((1, d, tn), lambda l, j: (l, 0, j)),
            pl.BlockSpec((1, 1, tn), lambda l, j: (l, 0, j)),
        ],
        out_specs=pl.BlockSpec((1, rows, tn), lambda l, j: (l, 0, j)),
        out_shape=jax.ShapeDtypeStruct((depth, rows, n), F32),
        compiler_params=pltpu.CompilerParams(
            dimension_semantics=("arbitrary", "arbitrary"), vmem_limit_bytes=VMEM_LIMIT_BYTES),
        name="ada",
    )(c_pad, ada_w, ada_b.reshape(depth, 1, n))
    return out[:, :b]


def _norm_proj_kernel(x_ref, g_ref, sc_ref, sh_ref, w_ref, o_ref, h_ref):
    @pl.when(pl.program_id(1) == 0)
    def _():
        h = (_rms(x_ref[...]) * g_ref[...]) * (1.0 + sc_ref[0]) + sh_ref[0]
        h_ref[...] = h.astype(BF16)

    o_ref[...] = _dot(h_ref[...], w_ref[...]).astype(o_ref.dtype)


def _norm_proj(x, g, sc, sh, w, seq, tm=1024, tn=512):
    m, d = x.shape
    n = w.shape[1]
    tm = min(tm, seq)
    per_seq = seq // tm
    return pl.pallas_call(
        _norm_proj_kernel,
        grid=(m // tm, n // tn),
        in_specs=[
            pl.BlockSpec((tm, d), lambda i, j: (i, 0)),
            pl.BlockSpec((1, d), lambda i, j: (0, 0)),
            pl.BlockSpec((1, 1, d), lambda i, j: (i // per_seq, 0, 0)),
            pl.BlockSpec((1, 1, d), lambda i, j: (i // per_seq, 0, 0)),
            pl.BlockSpec((d, tn), lambda i, j: (0, j)),
        ],
        out_specs=pl.BlockSpec((tm, tn), lambda i, j: (i, j)),
        out_shape=jax.ShapeDtypeStruct((m, n), BF16),
        scratch_shapes=[pltpu.VMEM((tm, d), BF16)],
        compiler_params=pltpu.CompilerParams(
            dimension_semantics=("parallel", "arbitrary"), vmem_limit_bytes=VMEM_LIMIT_BYTES),
        name="norm_proj",
    )(x, g.reshape(1, d), sc, sh, w)


def _out_proj_kernel(a_ref, w_ref, x_ref, g_ref, gate_ref, o_ref):
    y = _dot(a_ref[...], w_ref[...])
    o_ref[...] = x_ref[...] + gate_ref[0] * (_rms(y) * g_ref[...])


def _out_proj(a, w, x, g, gate, seq, tm=512):
    m, k = a.shape
    d = w.shape[1]
    tm = min(tm, seq)
    per_seq = seq // tm
    return pl.pallas_call(
        _out_proj_kernel,
        grid=(m // tm,),
        in_specs=[
            pl.BlockSpec((tm, k), lambda i: (i, 0)),
            pl.BlockSpec((k, d), lambda i: (0, 0)),
            pl.BlockSpec((tm, d), lambda i: (i, 0)),
            pl.BlockSpec((1, d), lambda i: (0, 0)),
            pl.BlockSpec((1, 1, d), lambda i: (i // per_seq, 0, 0)),
        ],
        out_specs=pl.BlockSpec((tm, d), lambda i: (i, 0)),
        out_shape=jax.ShapeDtypeStruct((m, d), F32),
        compiler_params=pltpu.CompilerParams(
            dimension_semantics=("parallel",), vmem_limit_bytes=VMEM_LIMIT_BYTES),
        name="out_proj",
    )(a, w, x, g.reshape(1, d), gate)


def _ffn_kernel(x_ref, xp_ref, xn_ref, g_ref, sc_ref, sh_ref, wg_ref, wu_ref, cw_ref, cb_ref,
                wd_ref, g2_ref, gate_ref, o_ref, h_ref, acc_ref, *, tm, per_seq):
    i = pl.program_id(0)
    f = pl.program_id(1)
    halo = FFN_HALO
    d = x_ref.shape[1]

    @pl.when(f == 0)
    def _():
        def modnorm(x):
            return (_rms(x) * g_ref[...]) * (1.0 + sc_ref[0]) + sh_ref[0]

        keep_prev = jnp.where(i % per_seq == 0, 0.0, 1.0)
        keep_next = jnp.where(i % per_seq == per_seq - 1, 0.0, 1.0)
        zeros = jnp.zeros((halo - SUBLANES_F32, d), F32)
        hp = modnorm(xp_ref[...]) * keep_prev
        hn = modnorm(xn_ref[...]) * keep_next
        h_ref[0:halo, :] = jnp.concatenate([zeros, hp], axis=0).astype(BF16)
        h_ref[halo:halo + tm, :] = modnorm(x_ref[...]).astype(BF16)
        h_ref[halo + tm:, :] = jnp.concatenate([hn, zeros], axis=0).astype(BF16)
        acc_ref[...] = jnp.zeros_like(acc_ref)

    rows = tm + 2 * halo
    gate = _dot(h_ref[...], wg_ref[...])
    up = _dot(h_ref[halo:halo + tm, :], wu_ref[...])
    g_prev = pltpu.roll(gate, 1, 0)[halo:halo + tm]
    g_next = pltpu.roll(gate, rows - 1, 0)[halo:halo + tm]
    g_mid = gate[halo:halo + tm]
    cw = cw_ref[...]
    z = g_prev * cw[0:1] + g_mid * cw[1:2] + g_next * cw[2:3] + cb_ref[...]
    act = jax.nn.gelu(z, approximate=True) * up
    acc_ref[...] += _dot(act.astype(BF16), wd_ref[...])

    @pl.when(f == pl.num_programs(1) - 1)
    def _():
        o_ref[...] = x_ref[...] + gate_ref[0] * (_rms(acc_ref[...]) * g2_ref[...])


def _ffn(x, g, sc, sh, w_gate, w_up, conv_w, conv_b, w_down, g2, gate, seq, tm=512, tf=512):
    m, d = x.shape
    dff = w_gate.shape[1]
    tm = min(tm, seq)
    per_seq = seq // tm
    nblk8 = m // SUBLANES_F32
    r8 = tm // SUBLANES_F32
    return pl.pallas_call(
        functools.partial(_ffn_kernel, tm=tm, per_seq=per_seq),
        grid=(m // tm, dff // tf),
        in_specs=[
            pl.BlockSpec((tm, d), lambda i, f: (i, 0)),
            pl.BlockSpec((SUBLANES_F32, d), lambda i, f: (jnp.maximum(i * r8 - 1, 0), 0)),
            pl.BlockSpec((SUBLANES_F32, d), lambda i, f: (jnp.minimum((i + 1) * r8, nblk8 - 1), 0)),
            pl.BlockSpec((1, d), lambda i, f: (0, 0)),
            pl.BlockSpec((1, 1, d), lambda i, f: (i // per_seq, 0, 0)),
            pl.BlockSpec((1, 1, d), lambda i, f: (i // per_seq, 0, 0)),
            pl.BlockSpec((d, tf), lambda i, f: (0, f)),
            pl.BlockSpec((d, tf), lambda i, f: (0, f)),
            pl.BlockSpec((CONV_WIDTH, tf), lambda i, f: (0, f)),
            pl.BlockSpec((1, tf), lambda i, f: (0, f)),
            pl.BlockSpec((tf, d), lambda i, f: (f, 0)),
            pl.BlockSpec((1, d), lambda i, f: (0, 0)),
            pl.BlockSpec((1, 1, d), lambda i, f: (i // per_seq, 0, 0)),
        ],
        out_specs=pl.BlockSpec((tm, d), lambda i, f: (i, 0)),
        out_shape=jax.ShapeDtypeStruct((m, d), F32),
        scratch_shapes=[pltpu.VMEM((tm + 2 * FFN_HALO, d), BF16), pltpu.VMEM((tm, d), F32)],
        compiler_params=pltpu.CompilerParams(
            dimension_semantics=("parallel", "arbitrary"), vmem_limit_bytes=VMEM_LIMIT_BYTES),
        name="ffn",
    )(x, x, x, g.reshape(1, d), sc, sh, w_gate, w_up, conv_w, conv_b.reshape(1, dff), w_down,
      g2.reshape(1, d), gate)


def _attn_ab_kernel(slopes_ref, sink_ref, qa_ref, qb0_ref, qb1_ref, qb2_ref, ka_ref, va_ref,
                    kb_ref, vb_ref, o_ref, *, seq):
    tq = AB_BLOCK_Q
    t0 = pl.program_id(1) * tq
    scale = HEAD_DIM ** -0.5
    qpos = t0 + lax.broadcasted_iota(jnp.int32, (tq, 1), 0)

    def window(span):
        width = min(tq + 2 * span, seq)
        start = jnp.clip(t0 - span, 0, seq - width)
        start = pl.multiple_of(start, SUBLANES_BF16)
        kpos = start + lax.broadcasted_iota(jnp.int32, (1, width), 1)
        dist = jnp.abs(qpos - kpos)
        return start, width, dist

    def head(ref, h):
        return ref[:, h * HEAD_DIM:(h + 1) * HEAD_DIM]

    start, width, dist = window(A_RADIUS)
    valid = dist <= A_RADIUS
    distf = dist.astype(F32)
    for kvh in range(A_KV_HEADS):
        cols = slice(kvh * HEAD_DIM, (kvh + 1) * HEAD_DIM)
        kw = ka_ref[pl.ds(start, width), cols]
        vw = va_ref[pl.ds(start, width), cols]
        for gq in range(A_GROUP):
            hq = kvh * A_GROUP + gq
            s = _dot_nt(head(qa_ref, hq), kw) * scale
            s = jnp.where(valid, s - slopes_ref[hq] * distf, NEG_INF)
            sink = sink_ref[hq]
            m = jnp.maximum(s.max(axis=-1, keepdims=True), sink)
            p = jnp.exp(s - m)
            den = p.sum(axis=-1, keepdims=True) + jnp.exp(sink - m)
            o = _dot(p.astype(BF16), vw) / den
            o_ref[:, hq * HEAD_DIM:(hq + 1) * HEAD_DIM] = o.astype(o_ref.dtype)

    qb_refs = (qb0_ref, qb1_ref, qb2_ref)
    geom = []
    for window_len, dil in B_PAIRS:
        span = (window_len // (2 * dil)) * dil
        start, width, dist = window(span)
        valid = (dist <= span) & ((dist & (dil - 1)) == 0)
        geom.append((start, width, valid, dist.astype(F32)))
    for hb in range(B_HEADS):
        cols = slice(hb * HEAD_DIM, (hb + 1) * HEAD_DIM)
        scores, vwins = [], []
        for gi in range(len(B_PAIRS)):
            start, width, valid, distf = geom[gi]
            kw = kb_ref[pl.ds(start, width), cols]
            vwins.append(vb_ref[pl.ds(start, width), cols])
            slope = slopes_ref[A_Q_HEADS + gi * B_HEADS + hb]
            s = _dot_nt(head(qb_refs[gi], hb), kw) * scale
            scores.append(jnp.where(valid, s - slope * distf, NEG_INF))
        m = functools.reduce(jnp.maximum, [s.max(axis=-1, keepdims=True) for s in scores])
        den = jnp.zeros((tq, 1), F32)
        acc = jnp.zeros((tq, HEAD_DIM), F32)
        for s, vw in zip(scores, vwins):
            p = jnp.exp(s - m)
            den = den + p.sum(axis=-1, keepdims=True)
            acc = acc + _dot(p.astype(BF16), vw)
        col = (A_Q_HEADS + hb) * HEAD_DIM
        o_ref[:, col:col + HEAD_DIM] = (acc / den).astype(o_ref.dtype)


def _attn_ab(proj, sink, batch, seq):
    tq = min(AB_BLOCK_Q, seq)
    nq = seq // tq
    wq = A_Q_HEADS * HEAD_DIM
    wka = A_KV_HEADS * HEAD_DIM
    wkb = B_HEADS * HEAD_DIM
    off_ka = wq
    off_qb = wq + 2 * wka
    off_kb = off_qb + len(B_PAIRS) * wkb
    assert all(d & (d - 1) == 0 for _, d in B_PAIRS)
    assert off_ka % wka == 0 and off_qb % wq == 0 and off_kb % wkb == 0
    slopes = jnp.asarray(_alibi_slopes(N_MIX_HEADS))
    smem = pl.BlockSpec(memory_space=pltpu.SMEM)

    def qspec(col_block):
        return pl.BlockSpec((tq, wq), lambda b, n: (b * nq + n, col_block))

    def kvspec(width, col_block):
        return pl.BlockSpec((seq, width), lambda b, n: (b, col_block))

    return pl.pallas_call(
        functools.partial(_attn_ab_kernel, seq=seq),
        grid=(batch, nq),
        in_specs=[
            smem, smem,
            qspec(0), qspec(off_qb // wq), qspec(off_qb // wq + 1), qspec(off_qb // wq + 2),
            kvspec(wka, off_ka // wka), kvspec(wka, off_ka // wka + 1),
            kvspec(wkb, off_kb // wkb), kvspec(wkb, off_kb // wkb + 1),
        ],
        out_specs=pl.BlockSpec((tq, wq + wkb), lambda b, n: (b * nq + n, 0)),
        out_shape=jax.ShapeDtypeStruct((batch * seq, wq + wkb), BF16),
        compiler_params=pltpu.CompilerParams(
            dimension_semantics=("parallel", "arbitrary"), vmem_limit_bytes=VMEM_LIMIT_BYTES),
        name="attn_ab",
    )(slopes, sink.astype(F32), proj, proj, proj, proj, proj, proj, proj, proj)


def _attn_c_kernel(slopes_ref, q_ref, k_ref, v_ref, lam_ref, sg_ref, o_ref, m_ref, l_ref, acc_ref,
                   *, tq, tk, lambda_init):
    h = pl.program_id(1)
    qi = pl.program_id(2)
    ki = pl.program_id(3)
    scale = HEAD_DIM ** -0.5

    @pl.when(ki == 0)
    def _():
        m_ref[...] = jnp.full_like(m_ref, NEG_INF)
        l_ref[...] = jnp.zeros_like(l_ref)
        acc_ref[...] = jnp.zeros_like(acc_ref)

    qpos = qi * tq + lax.broadcasted_iota(jnp.int32, (tq, 1), 0)
    kpos = ki * tk + lax.broadcasted_iota(jnp.int32, (1, tk), 1)
    bias = slopes_ref[h] * jnp.abs(qpos - kpos).astype(F32)
    v = v_ref[...]
    for j in range(2):
        cols = slice(j * HEAD_DIM, (j + 1) * HEAD_DIM)
        s = _dot_nt(q_ref[:, cols], k_ref[:, cols]) * scale - bias
        m_prev = m_ref[j]
        m_new = jnp.maximum(m_prev, s.max(axis=-1, keepdims=True))
        alpha = jnp.exp(m_prev - m_new)
        p = jnp.exp(s - m_new)
        l_ref[j] = alpha * l_ref[j] + p.sum(axis=-1, keepdims=True)
        acc_ref[j] = alpha * acc_ref[j] + _dot(p.astype(BF16), v)
        m_ref[j] = m_new

    @pl.when(ki == pl.num_programs(3) - 1)
    def _():
        lp = lam_ref[...]
        lam = (jnp.exp(jnp.sum(lp[0:1] * lp[1:2], axis=-1, keepdims=True))
               - jnp.exp(jnp.sum(lp[2:3] * lp[3:4], axis=-1, keepdims=True)) + lambda_init)
        o = acc_ref[0] / l_ref[0] - lam * (acc_ref[1] / l_ref[1])
        o = (_rms(o) * sg_ref[...]) * (1.0 - lambda_init)
        o_ref[...] = o.astype(o_ref.dtype)


def _attn_c(proj, lam_params, subln_g, lambda_init, batch, seq, tq=512, tk=512):
    wh = 2 * HEAD_DIM
    d = C_HEADS * wh
    tq = min(tq, seq)
    tk = min(tk, seq)
    nq, nk = seq // tq, seq // tk
    slopes = jnp.asarray(_alibi_slopes(C_HEADS))
    return pl.pallas_call(
        functools.partial(_attn_c_kernel, tq=tq, tk=tk, lambda_init=lambda_init),
        grid=(batch, C_HEADS, nq, nk),
        in_specs=[
            pl.BlockSpec(memory_space=pltpu.SMEM),
            pl.BlockSpec((tq, wh), lambda b, h, qi, ki: (b * nq + qi, h)),
            pl.BlockSpec((tk, wh), lambda b, h, qi, ki: (b * nk + ki, C_HEADS + h)),
            pl.BlockSpec((tk, wh), lambda b, h, qi, ki: (b * nk + ki, 2 * C_HEADS + h)),
            pl.BlockSpec((4, HEAD_DIM), lambda b, h, qi, ki: (0, 0)),
            pl.BlockSpec((1, wh), lambda b, h, qi, ki: (0, 0)),
        ],
        out_specs=pl.BlockSpec((tq, wh), lambda b, h, qi, ki: (b * nq + qi, h)),
        out_shape=jax.ShapeDtypeStruct((batch * seq, d), BF16),
        scratch_shapes=[pltpu.VMEM((2, tq, 1), F32), pltpu.VMEM((2, tq, 1), F32),
                        pltpu.VMEM((2, tq, wh), F32)],
        compiler_params=pltpu.CompilerParams(
            dimension_semantics=("parallel", "parallel", "parallel", "arbitrary"),
            vmem_limit_bytes=VMEM_LIMIT_BYTES),
        name="attn_c",
    )(slopes, proj, proj, proj, lam_params.astype(F32), subln_g.reshape(1, wh).astype(F32))


def _pad_cols(w, n):
    return jnp.pad(w, ((0, 0), (0, n - w.shape[1])))


def kernel(x, c, ada_w, ada_b, norm_g, ab_w_in, ab_w_out, a_sink, c_w_in, c_w_out, c_lambda,
           c_subln_g, ffn_w_up, ffn_conv_w, ffn_conv_b, ffn_w_down):
    batch, seq, d = x.shape
    depth = ada_w.shape[0]
    dff = ffn_w_down.shape[1]
    ffn_tf = 512
    dff_pad = -(-dff // ffn_tf) * ffn_tf

    mod = _ada(c, ada_w, ada_b)
    xs = x.reshape(batch * seq, d)
    for layer in range(depth):
        sh1, sc1, g1, sh2, sc2, g2 = [mod[layer, :, k * d:(k + 1) * d].reshape(batch, 1, d)
                                      for k in range(6)]
        j = layer // 2
        if layer % 2 == 0:
            proj = _norm_proj(xs, norm_g[layer, 0], sc1, sh1, ab_w_in[j].astype(BF16), seq)
            mixed = _attn_ab(proj, a_sink[j], batch, seq)
            w_out = ab_w_out[j]
        else:
            lambda_init = 0.8 - 0.6 * math.exp(-0.3 * layer)
            proj = _norm_proj(xs, norm_g[layer, 0], sc1, sh1, c_w_in[j].astype(BF16), seq)
            mixed = _attn_c(proj, c_lambda[j], c_subln_g[j], lambda_init, batch, seq)
            w_out = c_w_out[j]
        xs = _out_proj(mixed, w_out.astype(BF16), xs, norm_g[layer, 1], g1, seq)

        w_up = ffn_w_up[layer]
        w_gate = _pad_cols(w_up[:, :dff], dff_pad).astype(BF16)
        w_upp = _pad_cols(w_up[:, dff:], dff_pad).astype(BF16)
        conv_w = _pad_cols(ffn_conv_w[layer], dff_pad)
        conv_b = jnp.pad(ffn_conv_b[layer], (0, dff_pad - dff))
        w_down = jnp.pad(ffn_w_down[layer], ((0, dff_pad - dff), (0, 0))).astype(BF16)
        xs = _ffn(xs, norm_g[layer, 2], sc2, sh2, w_gate, w_upp, conv_w, conv_b, w_down,
                  norm_g[layer, 3], g2, seq, tf=ffn_tf)
    return xs.reshape(batch, seq, d)
```

```python
import functools
import math

import numpy as np
import jax
import jax.numpy as jnp
from jax import lax
from jax.experimental import pallas as pl
from jax.experimental.pallas import tpu as pltpu

F32 = jnp.float32
BF16 = jnp.bfloat16

HEAD_DIM = 128
A_Q_HEADS = 4
A_KV_HEADS = 2
A_GROUP = A_Q_HEADS // A_KV_HEADS
A_RADIUS = 128
B_PAIRS = ((128, 1), (512, 4), (2048, 16))
B_HEADS = 4
N_MIX_HEADS = A_Q_HEADS + len(B_PAIRS) * B_HEADS
C_HEADS = 8
CONV_WIDTH = 3
EPS = 1e-6
NEG_INF = -1e30

SUBLANES_F32 = 8
SUBLANES_BF16 = 16
LANES = 128
VMEM_LIMIT_BYTES = 56 * 1024 * 1024

AB_BLOCK_Q = 128
FFN_HALO = SUBLANES_BF16
FFN_TF = 512


def _alibi_slopes(n):
    return (2.0 ** (-8.0 * np.arange(1, n + 1, dtype=np.float32) / n)).astype(np.float32)


def _rms(x):
    return x * lax.rsqrt(jnp.mean(x * x, axis=-1, keepdims=True) + EPS)


def _dot(a, b):
    return jnp.dot(a, b, preferred_element_type=F32)


def _dot_nt(a, b):
    return lax.dot_general(a, b, (((1,), (1,)), ((), ())), preferred_element_type=F32)


def _ada_kernel(c_ref, w_ref, b_ref, o_ref):
    c = c_ref[...]
    cond = c * jax.nn.sigmoid(c)
    o_ref[0] = jnp.dot(cond, w_ref[0], preferred_element_type=F32,
                       precision=lax.Precision.HIGHEST) + b_ref[0]


def _ada(c, ada_w, ada_b, tn=1024):
    depth, d, n = ada_w.shape
    b = c.shape[0]
    rows = -(-b // SUBLANES_F32) * SUBLANES_F32
    c_pad = jnp.zeros((rows, d), F32).at[:b].set(c)
    out = pl.pallas_call(
        _ada_kernel,
        grid=(depth, n // tn),
        in_specs=[
            pl.BlockSpec((rows, d), lambda l, j: (0, 0)),
            pl.BlockSpec((1, d, tn), lambda l, j: (l, 0, j)),
            pl.BlockSpec((1, 1, tn), lambda l, j: (l, 0, j)),
        ],
        out_specs=pl.BlockSpec((1, rows, tn), lambda l, j: (l, 0, j)),
        out_shape=jax.ShapeDtypeStruct((depth, rows, n), F32),
        compiler_params=pltpu.CompilerParams(
            dimension_semantics=("arbitrary", "arbitrary"), vmem_limit_bytes=VMEM_LIMIT_BYTES),
        name="ada",
    )(c_pad, ada_w, ada_b.reshape(depth, 1, n))
    return out[:, :b]


def _norm_proj_kernel(x_ref, g_ref, sc_ref, sh_ref, w_ref, cs_ref, o_ref, h_ref):
    @pl.when(pl.program_id(1) == 0)
    def _():
        h = (_rms(x_ref[...]) * g_ref[...]) * (1.0 + sc_ref[0]) + sh_ref[0]
        h_ref[...] = h.astype(BF16)

    o_ref[...] = (_dot(h_ref[...], w_ref[...]) * cs_ref[...]).astype(o_ref.dtype)


def _norm_proj(x, g, sc, sh, w, col_scale, seq, tm=1024, tn=512):
    m, d = x.shape
    n = w.shape[1]
    tm = min(tm, seq)
    per_seq = seq // tm
    return pl.pallas_call(
        _norm_proj_kernel,
        grid=(m // tm, n // tn),
        in_specs=[
            pl.BlockSpec((tm, d), lambda i, j: (i, 0)),
            pl.BlockSpec((1, d), lambda i, j: (0, 0)),
            pl.BlockSpec((1, 1, d), lambda i, j: (i // per_seq, 0, 0)),
            pl.BlockSpec((1, 1, d), lambda i, j: (i // per_seq, 0, 0)),
            pl.BlockSpec((d, tn), lambda i, j: (0, j)),
            pl.BlockSpec((1, tn), lambda i, j: (0, j)),
        ],
        out_specs=pl.BlockSpec((tm, tn), lambda i, j: (i, j)),
        out_shape=jax.ShapeDtypeStruct((m, n), BF16),
        scratch_shapes=[pltpu.VMEM((tm, d), BF16)],
        compiler_params=pltpu.CompilerParams(
            dimension_semantics=("parallel", "arbitrary"), vmem_limit_bytes=VMEM_LIMIT_BYTES),
        name="norm_proj",
    )(x, g.reshape(1, d), sc, sh, w, col_scale.reshape(1, n))


def _out_proj_kernel(a_ref, w_ref, x_ref, g_ref, gate_ref, o_ref):
    y = _dot(a_ref[...], w_ref[...])
    o_ref[...] = x_ref[...] + gate_ref[0] * (_rms(y) * g_ref[...])


def _out_proj(a, w, x, g, gate, seq, tm=512):
    m, k = a.shape
    d = w.shape[1]
    tm = min(tm, seq)
    per_seq = seq // tm
    return pl.pallas_call(
        _out_proj_kernel,
        grid=(m // tm,),
        in_specs=[
            pl.BlockSpec((tm, k), lambda i: (i, 0)),
            pl.BlockSpec((k, d), lambda i: (0, 0)),
            pl.BlockSpec((tm, d), lambda i: (i, 0)),
            pl.BlockSpec((1, d), lambda i: (0, 0)),
            pl.BlockSpec((1, 1, d), lambda i: (i // per_seq, 0, 0)),
        ],
        out_specs=pl.BlockSpec((tm, d), lambda i: (i, 0)),
        out_shape=jax.ShapeDtypeStruct((m, d), F32),
        compiler_params=pltpu.CompilerParams(
            dimension_semantics=("parallel",), vmem_limit_bytes=VMEM_LIMIT_BYTES),
        name="out_proj",
    )(a, w, x, g.reshape(1, d), gate)


def _ffn_kernel(x_ref, xp_ref, xn_ref, g_ref, sc_ref, sh_ref, wg_ref, wu_ref, cw_ref, cb_ref,
                wd_ref, g2_ref, gate_ref, o_ref, h_ref, acc_ref, *, tm, per_seq):
    i = pl.program_id(0)
    f = pl.program_id(1)
    halo = FFN_HALO
    d = x_ref.shape[1]

    @pl.when(f == 0)
    def _():
        def modnorm(x):
            return (_rms(x) * g_ref[...]) * (1.0 + sc_ref[0]) + sh_ref[0]

        keep_prev = jnp.where(i % per_seq == 0, 0.0, 1.0)
        keep_next = jnp.where(i % per_seq == per_seq - 1, 0.0, 1.0)
        zeros = jnp.zeros((halo - SUBLANES_F32, d), F32)
        hp = modnorm(xp_ref[...]) * keep_prev
        hn = modnorm(xn_ref[...]) * keep_next
        h_ref[0:halo, :] = jnp.concatenate([zeros, hp], axis=0).astype(BF16)
        h_ref[halo:halo + tm, :] = modnorm(x_ref[...]).astype(BF16)
        h_ref[halo + tm:, :] = jnp.concatenate([hn, zeros], axis=0).astype(BF16)
        acc_ref[...] = jnp.zeros_like(acc_ref)

    rows = tm + 2 * halo
    gate = _dot(h_ref[...], wg_ref[...])
    up = _dot(h_ref[halo:halo + tm, :], wu_ref[...])
    g_prev = pltpu.roll(gate, 1, 0)[halo:halo + tm]
    g_next = pltpu.roll(gate, rows - 1, 0)[halo:halo + tm]
    g_mid = gate[halo:halo + tm]
    cw = cw_ref[...]
    z = g_prev * cw[0:1] + g_mid * cw[1:2] + g_next * cw[2:3] + cb_ref[...]
    act = jax.nn.gelu(z, approximate=True) * up
    acc_ref[...] += _dot(act.astype(BF16), wd_ref[...])

    @pl.when(f == pl.num_programs(1) - 1)
    def _():
        o_ref[...] = x_ref[...] + gate_ref[0] * (_rms(acc_ref[...]) * g2_ref[...])


def _ffn(x, g, sc, sh, w_gate, w_up, conv_w, conv_b, w_down, g2, gate, seq, tm=1024, tf=FFN_TF):
    m, d = x.shape
    dff = w_gate.shape[1]
    tm = min(tm, seq)
    per_seq = seq // tm
    nblk8 = m // SUBLANES_F32
    r8 = tm // SUBLANES_F32
    once = pl.Buffered(1)
    return pl.pallas_call(
        functools.partial(_ffn_kernel, tm=tm, per_seq=per_seq),
        grid=(m // tm, dff // tf),
        in_specs=[
            pl.BlockSpec((tm, d), lambda i, f: (i, 0), pipeline_mode=once),
            pl.BlockSpec((SUBLANES_F32, d), lambda i, f: (jnp.maximum(i * r8 - 1, 0), 0)),
            pl.BlockSpec((SUBLANES_F32, d), lambda i, f: (jnp.minimum((i + 1) * r8, nblk8 - 1), 0)),
            pl.BlockSpec((1, d), lambda i, f: (0, 0)),
            pl.BlockSpec((1, 1, d), lambda i, f: (i // per_seq, 0, 0)),
            pl.BlockSpec((1, 1, d), lambda i, f: (i // per_seq, 0, 0)),
            pl.BlockSpec((d, tf), lambda i, f: (0, f)),
            pl.BlockSpec((d, tf), lambda i, f: (0, f)),
            pl.BlockSpec((CONV_WIDTH, tf), lambda i, f: (0, f)),
            pl.BlockSpec((1, tf), lambda i, f: (0, f)),
            pl.BlockSpec((tf, d), lambda i, f: (f, 0)),
            pl.BlockSpec((1, d), lambda i, f: (0, 0)),
            pl.BlockSpec((1, 1, d), lambda i, f: (i // per_seq, 0, 0)),
        ],
        out_specs=pl.BlockSpec((tm, d), lambda i, f: (i, 0), pipeline_mode=once),
        out_shape=jax.ShapeDtypeStruct((m, d), F32),
        scratch_shapes=[pltpu.VMEM((tm + 2 * FFN_HALO, d), BF16), pltpu.VMEM((tm, d), F32)],
        compiler_params=pltpu.CompilerParams(
            dimension_semantics=("parallel", "arbitrary"), vmem_limit_bytes=VMEM_LIMIT_BYTES),
        name="ffn",
    )(x, x, x, g.reshape(1, d), sc, sh, w_gate, w_up, conv_w, conv_b.reshape(1, dff), w_down,
      g2.reshape(1, d), gate)


def _attn_ab_kernel(slopes_ref, sink_ref, qa_ref, qb0_ref, qb1_ref, qb2_ref, ka_ref, va_ref,
                    kb_ref, vb_ref, o_ref, *, seq):
    tq = AB_BLOCK_Q
    t0 = pl.program_id(1) * tq
    scale = HEAD_DIM ** -0.5
    qpos = t0 + lax.broadcasted_iota(jnp.int32, (tq, 1), 0)

    def window(span):
        width = min(tq + 2 * span, seq)
        start = jnp.clip(t0 - span, 0, seq - width)
        start = pl.multiple_of(start, SUBLANES_BF16)
        kpos = start + lax.broadcasted_iota(jnp.int32, (1, width), 1)
        dist = jnp.abs(qpos - kpos)
        return start, width, dist

    def head(ref, h):
        return ref[:, h * HEAD_DIM:(h + 1) * HEAD_DIM]

    start, width, dist = window(A_RADIUS)
    valid = dist <= A_RADIUS
    distf = dist.astype(F32)
    for kvh in range(A_KV_HEADS):
        cols = slice(kvh * HEAD_DIM, (kvh + 1) * HEAD_DIM)
        kw = ka_ref[pl.ds(start, width), cols]
        vw = va_ref[pl.ds(start, width), cols]
        for gq in range(A_GROUP):
            hq = kvh * A_GROUP + gq
            s = _dot_nt(head(qa_ref, hq), kw) * scale
            s = jnp.where(valid, s - slopes_ref[hq] * distf, NEG_INF)
            sink = sink_ref[hq]
            m = jnp.maximum(s.max(axis=-1, keepdims=True), sink)
            p = jnp.exp(s - m)
            den = p.sum(axis=-1, keepdims=True) + jnp.exp(sink - m)
            o = _dot(p.astype(BF16), vw) / den
            o_ref[:, hq * HEAD_DIM:(hq + 1) * HEAD_DIM] = o.astype(o_ref.dtype)

    qb_refs = (qb0_ref, qb1_ref, qb2_ref)
    geom = []
    for window_len, dil in B_PAIRS:
        span = (window_len // (2 * dil)) * dil
        start, width, dist = window(span)
        valid = (dist <= span) & ((dist & (dil - 1)) == 0)
        geom.append((start, width, valid, dist.astype(F32)))
    for hb in range(B_HEADS):
        cols = slice(hb * HEAD_DIM, (hb + 1) * HEAD_DIM)
        scores, vwins = [], []
        for gi in range(len(B_PAIRS)):
            start, width, valid, distf = geom[gi]
            kw = kb_ref[pl.ds(start, width), cols]
            vwins.append(vb_ref[pl.ds(start, width), cols])
            slope = slopes_ref[A_Q_HEADS + gi * B_HEADS + hb]
            s = _dot_nt(head(qb_refs[gi], hb), kw) * scale
            scores.append(jnp.where(valid, s - slope * distf, NEG_INF))
        m = functools.reduce(jnp.maximum, [s.max(axis=-1, keepdims=True) for s in scores])
        den = jnp.zeros((tq, 1), F32)
        acc = jnp.zeros((tq, HEAD_DIM), F32)
        for s, vw in zip(scores, vwins):
            p = jnp.exp(s - m)
            den = den + p.sum(axis=-1, keepdims=True)
            acc = acc + _dot(p.astype(BF16), vw)
        col = (A_Q_HEADS + hb) * HEAD_DIM
        o_ref[:, col:col + HEAD_DIM] = (acc / den).astype(o_ref.dtype)


def _attn_ab(proj, sink, batch, seq):
    tq = min(AB_BLOCK_Q, seq)
    nq = seq // tq
    wq = A_Q_HEADS * HEAD_DIM
    wka = A_KV_HEADS * HEAD_DIM
    wkb = B_HEADS * HEAD_DIM
    off_ka = wq
    off_qb = wq + 2 * wka
    off_kb = off_qb + len(B_PAIRS) * wkb
    assert all(d & (d - 1) == 0 for _, d in B_PAIRS)
    assert off_ka % wka == 0 and off_qb % wq == 0 and off_kb % wkb == 0
    slopes = jnp.asarray(_alibi_slopes(N_MIX_HEADS))
    smem = pl.BlockSpec(memory_space=pltpu.SMEM)

    def qspec(col_block):
        return pl.BlockSpec((tq, wq), lambda b, n: (b * nq + n, col_block))

    def kvspec(width, col_block):
        return pl.BlockSpec((seq, width), lambda b, n: (b, col_block))

    return pl.pallas_call(
        functools.partial(_attn_ab_kernel, seq=seq),
        grid=(batch, nq),
        in_specs=[
            smem, smem,
            qspec(0), qspec(off_qb // wq), qspec(off_qb // wq + 1), qspec(off_qb // wq + 2),
            kvspec(wka, off_ka // wka), kvspec(wka, off_ka // wka + 1),
            kvspec(wkb, off_kb // wkb), kvspec(wkb, off_kb // wkb + 1),
        ],
        out_specs=pl.BlockSpec((tq, wq + wkb), lambda b, n: (b * nq + n, 0)),
        out_shape=jax.ShapeDtypeStruct((batch * seq, wq + wkb), BF16),
        compiler_params=pltpu.CompilerParams(
            dimension_semantics=("parallel", "arbitrary"), vmem_limit_bytes=VMEM_LIMIT_BYTES),
        name="attn_ab",
    )(slopes, sink.astype(F32), proj, proj, proj, proj, proj, proj, proj, proj)


C_POS_BITS = 6
C_SLOPE_PARTS = 3
C_AUG_POS = 2 * C_SLOPE_PARTS
C_STRIP = 64


def _attn_c_consts():
    c = (_alibi_slopes(C_HEADS).astype(np.float64) * math.log2(math.e)).astype(np.float32)
    parts, rest = [], c.copy()
    for _ in range(C_SLOPE_PARTS):
        p = rest.astype(BF16).astype(np.float32)
        parts.append(p)
        rest = rest - p
    qc = np.zeros((C_HEADS, 1, LANES), np.float32)
    kc = np.zeros((C_HEADS, 1, LANES), np.float32)
    for a, p in enumerate(parts):
        qc[:, 0, C_AUG_POS + 2 * a] = p
        qc[:, 0, C_AUG_POS + 2 * a + 1] = p
        kc[:, 0, 2 * a] = p * (1 << C_POS_BITS)
        kc[:, 0, 2 * a + 1] = p
    return jnp.asarray(c), jnp.asarray(qc), jnp.asarray(kc)


def _attn_c_kernel(c_ref, qc_ref, kc_ref, q_ref, k_ref, v_ref, lam_ref, sg_ref, o_ref,
                   kaug_ref, qv_ref, bias_ref, u_ref, p_ref, m_ref, l_ref, alpha_ref, acc_ref,
                   *, t, seq, lambda_init):
    h = pl.program_id(1)
    qi = pl.program_id(2)
    nk = seq // t
    lane = lax.broadcasted_iota(jnp.int32, (1, LANES), 1)
    hi_lane = (lane & 1) == 0
    lo_mask = (1 << C_POS_BITS) - 1

    @pl.when(qi == 0)
    def _():
        key_pos_lane = (lane >= C_AUG_POS) & (lane < 2 * C_AUG_POS)

        def fill(kb, carry):
            rows = pl.ds(pl.multiple_of(kb * t, t), t)
            pos = kb * t + lax.broadcasted_iota(jnp.int32, (t, 1), 0)
            hi = (pos - (pos & lo_mask)).astype(F32)
            lo = (pos & lo_mask).astype(F32)
            aug = jnp.where(key_pos_lane, -jnp.where(hi_lane, hi, lo), kc_ref[0]).astype(BF16)
            for j in range(2):
                kaug_ref[j, rows, 0:HEAD_DIM] = k_ref[rows, j * HEAD_DIM:(j + 1) * HEAD_DIM]
                kaug_ref[j, rows, HEAD_DIM:2 * HEAD_DIM] = aug
            return carry

        lax.fori_loop(0, nk, fill, 0)
        ii = lax.broadcasted_iota(jnp.int32, (t, 1), 0)
        jj = lax.broadcasted_iota(jnp.int32, (1, t), 1)
        bias_ref[...] = c_ref[h] * jnp.abs(ii - jj).astype(F32)

    qpos = qi * t + lax.broadcasted_iota(jnp.int32, (t, 1), 0)
    q_hi = (qpos >> C_POS_BITS).astype(F32)
    q_lo = (qpos & lo_mask).astype(F32)
    q_aug = jnp.where(lane < C_AUG_POS, jnp.where(hi_lane, q_hi, q_lo), qc_ref[0])
    for j in range(2):
        qj = q_ref[:, j * HEAD_DIM:(j + 1) * HEAD_DIM]
        for variant, aug in enumerate((-q_aug, q_aug, jnp.zeros_like(q_aug))):
            qv_ref[variant, j, :, 0:HEAD_DIM] = qj
            qv_ref[variant, j, :, HEAD_DIM:2 * HEAD_DIM] = aug.astype(BF16)

    def key_rows(step):
        kb = qi + step
        kb = jnp.where(kb >= nk, kb - nk, kb)
        return kb, pl.ds(pl.multiple_of(kb * t, t), t)

    def scores(step, slot):
        kb, rows = key_rows(step)
        variant = 2 if step == 0 else jnp.where(kb < qi, 0, 1)
        for j in range(2):
            u_ref[slot, j] = _dot_nt(qv_ref[variant, j], kaug_ref[j, rows, :])

    def accumulate(step, slot):
        _, rows = key_rows(step)
        vblk = v_ref[rows, :]
        for j in range(2):
            for r0 in range(0, t, C_STRIP):
                rs = slice(r0, r0 + C_STRIP)
                u = u_ref[slot, j, rs, :]
                if step == 0:
                    u = u - bias_ref[rs, :]
                    m_new = jnp.broadcast_to(jnp.max(u, axis=1, keepdims=True), (C_STRIP, LANES))
                else:
                    m_prev = m_ref[j, rs, :]
                    m_new = jnp.maximum(m_prev, jnp.max(u, axis=1, keepdims=True))
                    alpha = jnp.exp2(m_prev - m_new)
                    alpha_ref[j, rs, :] = alpha
                p = jnp.exp2(u - jnp.tile(m_new, (1, t // LANES)))
                psum = p[:, 0:LANES]
                for cb in range(1, t // LANES):
                    psum = psum + p[:, cb * LANES:(cb + 1) * LANES]
                l_ref[j, rs, :] = psum if step == 0 else alpha * l_ref[j, rs, :] + psum
                m_ref[j, rs, :] = m_new
                p_ref[slot, j, rs, :] = p.astype(BF16)
            pv = _dot(p_ref[slot, j], vblk)
            acc_ref[j] = pv if step == 0 else jnp.tile(alpha_ref[j], (1, 2)) * acc_ref[j] + pv

    scores(0, 0)
    for step in range(nk):
        if step + 1 < nk:
            scores(step + 1, (step + 1) % 2)
        accumulate(step, step % 2)

    lp = lam_ref[...]
    lam = (jnp.exp(jnp.sum(lp[0:1] * lp[1:2], axis=-1, keepdims=True))
           - jnp.exp(jnp.sum(lp[2:3] * lp[3:4], axis=-1, keepdims=True)) + lambda_init)
    den0 = jnp.sum(l_ref[0], axis=1, keepdims=True)
    den1 = jnp.sum(l_ref[1], axis=1, keepdims=True)
    o = acc_ref[0] / den0 - lam * (acc_ref[1] / den1)
    o = (_rms(o) * sg_ref[...]) * (1.0 - lambda_init)
    o_ref[...] = o.astype(o_ref.dtype)


def _attn_c(proj, lam_params, subln_g, lambda_init, batch, seq, t=512):
    wh = 2 * HEAD_DIM
    d = C_HEADS * wh
    t = min(t, seq)
    nq = seq // t
    assert seq <= (1 << (2 * C_POS_BITS)) and t % LANES == 0
    c, qc, kc = _attn_c_consts()
    const_spec = pl.BlockSpec((1, 1, LANES), lambda b, h, qi: (h, 0, 0))
    return pl.pallas_call(
        functools.partial(_attn_c_kernel, t=t, seq=seq, lambda_init=lambda_init),
        grid=(batch, C_HEADS, nq),
        in_specs=[
            pl.BlockSpec(memory_space=pltpu.SMEM),
            const_spec, const_spec,
            pl.BlockSpec((t, wh), lambda b, h, qi: (b * nq + qi, h)),
            pl.BlockSpec((seq, wh), lambda b, h, qi: (b, C_HEADS + h)),
            pl.BlockSpec((seq, wh), lambda b, h, qi: (b, 2 * C_HEADS + h)),
            pl.BlockSpec((4, HEAD_DIM), lambda b, h, qi: (0, 0)),
            pl.BlockSpec((1, wh), lambda b, h, qi: (0, 0)),
        ],
        out_specs=pl.BlockSpec((t, wh), lambda b, h, qi: (b * nq + qi, h)),
        out_shape=jax.ShapeDtypeStruct((batch * seq, d), BF16),
        scratch_shapes=[
            pltpu.VMEM((2, seq, wh), BF16),
            pltpu.VMEM((3, 2, t, wh), BF16),
            pltpu.VMEM((t, t), F32),
            pltpu.VMEM((2, 2, t, t), F32),
            pltpu.VMEM((2, 2, t, t), BF16),
            pltpu.VMEM((2, t, LANES), F32),
            pltpu.VMEM((2, t, LANES), F32),
            pltpu.VMEM((2, t, LANES), F32),
            pltpu.VMEM((2, t, wh), F32),
        ],
        compiler_params=pltpu.CompilerParams(
            dimension_semantics=("parallel", "parallel", "arbitrary"),
            vmem_limit_bytes=VMEM_LIMIT_BYTES),
        name="attn_c",
    )(c, qc, kc, proj, proj, proj, lam_params.astype(F32), subln_g.reshape(1, wh).astype(F32))


def _pad_cols(w, n):
    return jnp.pad(w, ((0, 0), (0, n - w.shape[1])))


def kernel(x, c, ada_w, ada_b, norm_g, ab_w_in, ab_w_out, a_sink, c_w_in, c_w_out, c_lambda,
           c_subln_g, ffn_w_up, ffn_conv_w, ffn_conv_b, ffn_w_down):
    batch, seq, d = x.shape
    depth = ada_w.shape[0]
    dff = ffn_w_down.shape[1]
    dff_pad = -(-dff // FFN_TF) * FFN_TF

    mod = _ada(c, ada_w, ada_b)
    xs = x.reshape(batch * seq, d)
    for layer in range(depth):
        sh1, sc1, g1, sh2, sc2, g2 = [mod[layer, :, k * d:(k + 1) * d].reshape(batch, 1, d)
                                      for k in range(6)]
        j = layer // 2
        if layer % 2 == 0:
            w_in = ab_w_in[j].astype(BF16)
            col_scale = jnp.ones((w_in.shape[1],), F32)
            proj = _norm_proj(xs, norm_g[layer, 0], sc1, sh1, w_in, col_scale, seq)
            mixed = _attn_ab(proj, a_sink[j], batch, seq)
            w_out = ab_w_out[j]
        else:
            lambda_init = 0.8 - 0.6 * math.exp(-0.3 * layer)
            w_in = c_w_in[j].astype(BF16)
            q_scale = np.float32(HEAD_DIM ** -0.5 * math.log2(math.e))
            col_scale = jnp.concatenate([jnp.full((d,), q_scale, F32), jnp.ones((2 * d,), F32)])
            proj = _norm_proj(xs, norm_g[layer, 0], sc1, sh1, w_in, col_scale, seq)
            mixed = _attn_c(proj, c_lambda[j], c_subln_g[j], lambda_init, batch, seq)
            w_out = c_w_out[j]
        xs = _out_proj(mixed, w_out.astype(BF16), xs, norm_g[layer, 1], g1, seq)

        w_up = ffn_w_up[layer]
        w_gate = _pad_cols(w_up[:, :dff], dff_pad).astype(BF16)
        w_upp = _pad_cols(w_up[:, dff:], dff_pad).astype(BF16)
        conv_w = _pad_cols(ffn_conv_w[layer], dff_pad)
        conv_b = jnp.pad(ffn_conv_b[layer], (0, dff_pad - dff))
        w_down = jnp.pad(ffn_w_down[layer], ((0, dff_pad - dff), (0, 0))).astype(BF16)
        xs = _ffn(xs, norm_g[layer, 2], sc2, sh2, w_gate, w_upp, conv_w, conv_b, w_down,
                  norm_g[layer, 3], g2, seq)
    return xs.reshape(batch, seq, d)
```

```python
import functools
import math

import numpy as np
import jax
import jax.numpy as jnp
from jax import lax
from jax.experimental import pallas as pl
from jax.experimental.pallas import tpu as pltpu

F32 = jnp.float32
BF16 = jnp.bfloat16

HEAD_DIM = 128
A_Q_HEADS = 4
A_KV_HEADS = 2
A_GROUP = A_Q_HEADS // A_KV_HEADS
A_RADIUS = 128
B_PAIRS = ((128, 1), (512, 4), (2048, 16))
B_HEADS = 4
N_MIX_HEADS = A_Q_HEADS + len(B_PAIRS) * B_HEADS
C_HEADS = 8
CONV_WIDTH = 3
EPS = 1e-6
NEG_INF = -1e30

SUBLANES_F32 = 8
SUBLANES_BF16 = 16
LANES = 128
VMEM_LIMIT_BYTES = 56 * 1024 * 1024

AB_BLOCK_Q = 128
FFN_HALO = SUBLANES_BF16
FFN_TF = 512


def _alibi_slopes(n):
    return (2.0 ** (-8.0 * np.arange(1, n + 1, dtype=np.float32) / n)).astype(np.float32)


def _rms(x):
    return x * lax.rsqrt(jnp.mean(x * x, axis=-1, keepdims=True) + EPS)


def _dot(a, b):
    return jnp.dot(a, b, preferred_element_type=F32)


def _dot_nt(a, b):
    return lax.dot_general(a, b, (((1,), (1,)), ((), ())), preferred_element_type=F32)


def _ada_kernel(c_ref, w_ref, b_ref, o_ref):
    c = c_ref[...]
    cond = c * jax.nn.sigmoid(c)
    o_ref[...] = jnp.dot(cond, w_ref[...], preferred_element_type=F32,
                         precision=lax.Precision.HIGHEST) + b_ref[...]


def _ada(c, ada_w, ada_b):
    depth, d, n = ada_w.shape
    b = c.shape[0]
    assert b <= SUBLANES_F32 and n % d == 0
    c_pad = jnp.zeros((SUBLANES_F32, d), F32).at[:b].set(c)
    return pl.pallas_call(
        _ada_kernel,
        grid=(depth, n // d),
        in_specs=[
            pl.BlockSpec((SUBLANES_F32, d), lambda l, k: (0, 0)),
            pl.BlockSpec((None, d, d), lambda l, k: (l, 0, k)),
            pl.BlockSpec((None, 1, d), lambda l, k: (l, 0, k)),
        ],
        out_specs=pl.BlockSpec((None, None, SUBLANES_F32, d), lambda l, k: (l, k, 0, 0)),
        out_shape=jax.ShapeDtypeStruct((depth, n // d, SUBLANES_F32, d), F32),
        compiler_params=pltpu.CompilerParams(
            dimension_semantics=("arbitrary", "arbitrary"), vmem_limit_bytes=VMEM_LIMIT_BYTES),
        name="ada",
    )(c_pad, ada_w, ada_b.reshape(depth, 1, n))


MOD_SH1, MOD_SC1, MOD_G1, MOD_SH2, MOD_SC2, MOD_G2 = range(6)
NORM_PRE_MIX, NORM_POST_MIX, NORM_PRE_FFN, NORM_POST_FFN = range(4)


def _mod_spec(layer, chunk, d):
    return pl.BlockSpec((None, None, SUBLANES_F32, d), lambda *_: (layer, chunk, 0, 0))


def _norm_spec(layer, d):
    return pl.BlockSpec((None, 4, d), lambda *_: (layer, 0, 0))


def _batch_row(ref, b):
    return ref[pl.ds(b, 1), :]


def _cast_pad_kernel(w_ref, o_ref, *, axis):
    n = w_ref.shape[axis]
    if axis == 0:
        o_ref[:n, :] = w_ref[...].astype(o_ref.dtype)
        o_ref[n:, :] = jnp.zeros((o_ref.shape[0] - n, o_ref.shape[1]), o_ref.dtype)
    else:
        o_ref[:, :n] = w_ref[...].astype(o_ref.dtype)
        o_ref[:, n:] = jnp.zeros((o_ref.shape[0], o_ref.shape[1] - n), o_ref.dtype)


def _prep_w_up(w_up, dff_pad, tr=256):
    depth, d, two_dff = w_up.shape
    dff = two_dff // 2
    return pl.pallas_call(
        functools.partial(_cast_pad_kernel, axis=1),
        grid=(depth, 2, d // tr),
        in_specs=[pl.BlockSpec((None, tr, dff), lambda l, p, r: (l, r, p))],
        out_specs=pl.BlockSpec((None, None, tr, dff_pad), lambda l, p, r: (l, p, r, 0)),
        out_shape=jax.ShapeDtypeStruct((depth, 2, d, dff_pad), BF16),
        compiler_params=pltpu.CompilerParams(
            dimension_semantics=("arbitrary",) * 3, vmem_limit_bytes=VMEM_LIMIT_BYTES),
        name="prep_w_up",
    )(w_up)


def _prep_w_down(w_down, dff_pad, tc=256):
    depth, dff, d = w_down.shape
    return pl.pallas_call(
        functools.partial(_cast_pad_kernel, axis=0),
        grid=(depth, d // tc),
        in_specs=[pl.BlockSpec((None, dff, tc), lambda l, c: (l, 0, c))],
        out_specs=pl.BlockSpec((None, dff_pad, tc), lambda l, c: (l, 0, c)),
        out_shape=jax.ShapeDtypeStruct((depth, dff_pad, d), BF16),
        compiler_params=pltpu.CompilerParams(
            dimension_semantics=("arbitrary",) * 2, vmem_limit_bytes=VMEM_LIMIT_BYTES),
        name="prep_w_down",
    )(w_down)


def _norm_proj_kernel(x_ref, g_ref, sc_ref, sh_ref, w_ref, cs_ref, o_ref, h_ref, *, per_seq):
    @pl.when(pl.program_id(1) == 0)
    def _():
        b = pl.program_id(0) // per_seq
        g = g_ref[NORM_PRE_MIX:NORM_PRE_MIX + 1, :]
        h = (_rms(x_ref[...]) * g) * (1.0 + _batch_row(sc_ref, b)) + _batch_row(sh_ref, b)
        h_ref[...] = h.astype(BF16)

    o_ref[...] = (_dot(h_ref[...], w_ref[...]) * cs_ref[...]).astype(o_ref.dtype)


def _norm_proj(x, norm_g, mod, layer, w, wl, col_scale, seq, tm=1024, tn=512):
    m, d = x.shape
    n = w.shape[2]
    tm = min(tm, seq)
    return pl.pallas_call(
        functools.partial(_norm_proj_kernel, per_seq=seq // tm),
        grid=(m // tm, n // tn),
        in_specs=[
            pl.BlockSpec((tm, d), lambda i, j: (i, 0)),
            _norm_spec(layer, d),
            _mod_spec(layer, MOD_SC1, d),
            _mod_spec(layer, MOD_SH1, d),
            pl.BlockSpec((None, d, tn), lambda i, j: (wl, 0, j)),
            pl.BlockSpec((1, tn), lambda i, j: (0, j)),
        ],
        out_specs=pl.BlockSpec((tm, tn), lambda i, j: (i, j)),
        out_shape=jax.ShapeDtypeStruct((m, n), BF16),
        scratch_shapes=[pltpu.VMEM((tm, d), BF16)],
        compiler_params=pltpu.CompilerParams(
            dimension_semantics=("parallel", "arbitrary"), vmem_limit_bytes=VMEM_LIMIT_BYTES),
        name="norm_proj",
    )(x, norm_g, mod, mod, w, col_scale.reshape(1, n))


def _out_proj_kernel(a_ref, w_ref, x_ref, g_ref, gate_ref, o_ref, *, per_seq):
    b = pl.program_id(0) // per_seq
    y = _dot(a_ref[...], w_ref[...])
    g = g_ref[NORM_POST_MIX:NORM_POST_MIX + 1, :]
    o_ref[...] = x_ref[...] + _batch_row(gate_ref, b) * (_rms(y) * g)


def _out_proj(a, w, wl, x, norm_g, mod, layer, seq, tm=512):
    m, k = a.shape
    d = w.shape[2]
    tm = min(tm, seq)
    return pl.pallas_call(
        functools.partial(_out_proj_kernel, per_seq=seq // tm),
        grid=(m // tm,),
        in_specs=[
            pl.BlockSpec((tm, k), lambda i: (i, 0)),
            pl.BlockSpec((None, k, d), lambda i: (wl, 0, 0)),
            pl.BlockSpec((tm, d), lambda i: (i, 0)),
            _norm_spec(layer, d),
            _mod_spec(layer, MOD_G1, d),
        ],
        out_specs=pl.BlockSpec((tm, d), lambda i: (i, 0)),
        out_shape=jax.ShapeDtypeStruct((m, d), F32),
        compiler_params=pltpu.CompilerParams(
            dimension_semantics=("parallel",), vmem_limit_bytes=VMEM_LIMIT_BYTES),
        name="out_proj",
    )(a, w, x, norm_g, mod)


def _ffn_kernel(x_ref, xp_ref, xn_ref, g_ref, sc_ref, sh_ref, wg_ref, wu_ref, cw_ref, cb_ref,
                wd_ref, gate_ref, o_ref, h_ref, acc_ref, *, tm, per_seq):
    i = pl.program_id(0)
    f = pl.program_id(1)
    b = i // per_seq
    halo = FFN_HALO
    d = x_ref.shape[1]

    @pl.when(f == 0)
    def _():
        g_mod = g_ref[NORM_PRE_FFN:NORM_PRE_FFN + 1, :] * (1.0 + _batch_row(sc_ref, b))
        shift = _batch_row(sh_ref, b)

        def modnorm(x):
            return _rms(x) * g_mod + shift

        keep_prev = jnp.where(i % per_seq == 0, 0.0, 1.0)
        keep_next = jnp.where(i % per_seq == per_seq - 1, 0.0, 1.0)
        zeros = jnp.zeros((halo - SUBLANES_F32, d), F32)
        hp = modnorm(xp_ref[...]) * keep_prev
        hn = modnorm(xn_ref[...]) * keep_next
        h_ref[0:halo, :] = jnp.concatenate([zeros, hp], axis=0).astype(BF16)
        h_ref[halo:halo + tm, :] = modnorm(x_ref[...]).astype(BF16)
        h_ref[halo + tm:, :] = jnp.concatenate([hn, zeros], axis=0).astype(BF16)
        acc_ref[...] = jnp.zeros_like(acc_ref)

    rows = tm + 2 * halo
    gate = _dot(h_ref[...], wg_ref[...])
    up = _dot(h_ref[halo:halo + tm, :], wu_ref[...])
    g_prev = pltpu.roll(gate, 1, 0)[halo:halo + tm]
    g_next = pltpu.roll(gate, rows - 1, 0)[halo:halo + tm]
    g_mid = gate[halo:halo + tm]
    cw = cw_ref[...]
    z = g_prev * cw[0:1] + g_mid * cw[1:2] + g_next * cw[2:3] + cb_ref[...]
    act = jax.nn.gelu(z, approximate=True) * up
    acc_ref[...] += _dot(act.astype(BF16), wd_ref[...])

    @pl.when(f == pl.num_programs(1) - 1)
    def _():
        g2 = g_ref[NORM_POST_FFN:NORM_POST_FFN + 1, :]
        o_ref[...] = x_ref[...] + _batch_row(gate_ref, b) * (_rms(acc_ref[...]) * g2)


def _ffn(x, norm_g, mod, layer, w_gate_up, conv_w, conv_b, w_down, seq, tm=1024, tf=FFN_TF):
    m, d = x.shape
    dff = w_down.shape[1]
    tm = min(tm, seq)
    per_seq = seq // tm
    nblk8 = m // SUBLANES_F32
    r8 = tm // SUBLANES_F32
    once = pl.Buffered(1)
    return pl.pallas_call(
        functools.partial(_ffn_kernel, tm=tm, per_seq=per_seq),
        grid=(m // tm, dff // tf),
        in_specs=[
            pl.BlockSpec((tm, d), lambda i, f: (i, 0), pipeline_mode=once),
            pl.BlockSpec((SUBLANES_F32, d), lambda i, f: (jnp.maximum(i * r8 - 1, 0), 0)),
            pl.BlockSpec((SUBLANES_F32, d), lambda i, f: (jnp.minimum((i + 1) * r8, nblk8 - 1), 0)),
            _norm_spec(layer, d),
            _mod_spec(layer, MOD_SC2, d),
            _mod_spec(layer, MOD_SH2, d),
            pl.BlockSpec((None, None, d, tf), lambda i, f: (layer, 0, 0, f)),
            pl.BlockSpec((None, None, d, tf), lambda i, f: (layer, 1, 0, f)),
            pl.BlockSpec((None, CONV_WIDTH, tf), lambda i, f: (layer, 0, f)),
            pl.BlockSpec((None, 1, tf), lambda i, f: (layer, 0, f)),
            pl.BlockSpec((None, tf, d), lambda i, f: (layer, f, 0)),
            _mod_spec(layer, MOD_G2, d),
        ],
        out_specs=pl.BlockSpec((tm, d), lambda i, f: (i, 0), pipeline_mode=once),
        out_shape=jax.ShapeDtypeStruct((m, d), F32),
        scratch_shapes=[pltpu.VMEM((tm + 2 * FFN_HALO, d), BF16), pltpu.VMEM((tm, d), F32)],
        compiler_params=pltpu.CompilerParams(
            dimension_semantics=("parallel", "arbitrary"), vmem_limit_bytes=VMEM_LIMIT_BYTES),
        name="ffn",
    )(x, x, x, norm_g, mod, mod, w_gate_up, w_gate_up, conv_w, conv_b, w_down, mod)


def _attn_ab_kernel(slopes_ref, sink_ref, qa_ref, qb0_ref, qb1_ref, qb2_ref, ka_ref, va_ref,
                    kb_ref, vb_ref, o_ref, *, seq, wl):
    tq = AB_BLOCK_Q
    t0 = pl.program_id(1) * tq
    scale = HEAD_DIM ** -0.5
    qpos = t0 + lax.broadcasted_iota(jnp.int32, (tq, 1), 0)

    def window(span):
        width = min(tq + 2 * span, seq)
        start = jnp.clip(t0 - span, 0, seq - width)
        start = pl.multiple_of(start, SUBLANES_BF16)
        kpos = start + lax.broadcasted_iota(jnp.int32, (1, width), 1)
        dist = jnp.abs(qpos - kpos)
        return start, width, dist

    def head(ref, h):
        return ref[:, h * HEAD_DIM:(h + 1) * HEAD_DIM]

    start, width, dist = window(A_RADIUS)
    valid = dist <= A_RADIUS
    distf = dist.astype(F32)
    for kvh in range(A_KV_HEADS):
        cols = slice(kvh * HEAD_DIM, (kvh + 1) * HEAD_DIM)
        kw = ka_ref[pl.ds(start, width), cols]
        vw = va_ref[pl.ds(start, width), cols]
        for gq in range(A_GROUP):
            hq = kvh * A_GROUP + gq
            s = _dot_nt(head(qa_ref, hq), kw) * scale
            s = jnp.where(valid, s - slopes_ref[hq] * distf, NEG_INF)
            sink = sink_ref[wl, hq]
            m = jnp.maximum(s.max(axis=-1, keepdims=True), sink)
            p = jnp.exp(s - m)
            den = p.sum(axis=-1, keepdims=True) + jnp.exp(sink - m)
            o = _dot(p.astype(BF16), vw) / den
            o_ref[:, hq * HEAD_DIM:(hq + 1) * HEAD_DIM] = o.astype(o_ref.dtype)

    qb_refs = (qb0_ref, qb1_ref, qb2_ref)
    geom = []
    for window_len, dil in B_PAIRS:
        span = (window_len // (2 * dil)) * dil
        start, width, dist = window(span)
        valid = (dist <= span) & ((dist & (dil - 1)) == 0)
        geom.append((start, width, valid, dist.astype(F32)))
    for hb in range(B_HEADS):
        cols = slice(hb * HEAD_DIM, (hb + 1) * HEAD_DIM)
        scores, vwins = [], []
        for gi in range(len(B_PAIRS)):
            start, width, valid, distf = geom[gi]
            kw = kb_ref[pl.ds(start, width), cols]
            vwins.append(vb_ref[pl.ds(start, width), cols])
            slope = slopes_ref[A_Q_HEADS + gi * B_HEADS + hb]
            s = _dot_nt(head(qb_refs[gi], hb), kw) * scale
            scores.append(jnp.where(valid, s - slope * distf, NEG_INF))
        m = functools.reduce(jnp.maximum, [s.max(axis=-1, keepdims=True) for s in scores])
        den = jnp.zeros((tq, 1), F32)
        acc = jnp.zeros((tq, HEAD_DIM), F32)
        for s, vw in zip(scores, vwins):
            p = jnp.exp(s - m)
            den = den + p.sum(axis=-1, keepdims=True)
            acc = acc + _dot(p.astype(BF16), vw)
        col = (A_Q_HEADS + hb) * HEAD_DIM
        o_ref[:, col:col + HEAD_DIM] = (acc / den).astype(o_ref.dtype)


def _attn_ab(proj, sink, wl, batch, seq):
    tq = min(AB_BLOCK_Q, seq)
    nq = seq // tq
    wq = A_Q_HEADS * HEAD_DIM
    wka = A_KV_HEADS * HEAD_DIM
    wkb = B_HEADS * HEAD_DIM
    off_ka = wq
    off_qb = wq + 2 * wka
    off_kb = off_qb + len(B_PAIRS) * wkb
    assert all(d & (d - 1) == 0 for _, d in B_PAIRS)
    assert off_ka % wka == 0 and off_qb % wq == 0 and off_kb % wkb == 0
    slopes = jnp.asarray(_alibi_slopes(N_MIX_HEADS))
    smem = pl.BlockSpec(memory_space=pltpu.SMEM)

    def qspec(col_block):
        return pl.BlockSpec((tq, wq), lambda b, n: (b * nq + n, col_block))

    def kvspec(width, col_block):
        return pl.BlockSpec((seq, width), lambda b, n: (b, col_block))

    return pl.pallas_call(
        functools.partial(_attn_ab_kernel, seq=seq, wl=wl),
        grid=(batch, nq),
        in_specs=[
            smem, smem,
            qspec(0), qspec(off_qb // wq), qspec(off_qb // wq + 1), qspec(off_qb // wq + 2),
            kvspec(wka, off_ka // wka), kvspec(wka, off_ka // wka + 1),
            kvspec(wkb, off_kb // wkb), kvspec(wkb, off_kb // wkb + 1),
        ],
        out_specs=pl.BlockSpec((tq, wq + wkb), lambda b, n: (b * nq + n, 0)),
        out_shape=jax.ShapeDtypeStruct((batch * seq, wq + wkb), BF16),
        compiler_params=pltpu.CompilerParams(
            dimension_semantics=("parallel", "arbitrary"), vmem_limit_bytes=VMEM_LIMIT_BYTES),
        name="attn_ab",
    )(slopes, sink, proj, proj, proj, proj, proj, proj, proj, proj)


C_POS_BITS = 6
C_SLOPE_PARTS = 3
C_AUG_POS = 2 * C_SLOPE_PARTS
C_STRIP = 64


def _attn_c_consts():
    c = (_alibi_slopes(C_HEADS).astype(np.float64) * math.log2(math.e)).astype(np.float32)
    parts, rest = [], c.copy()
    for _ in range(C_SLOPE_PARTS):
        p = rest.astype(BF16).astype(np.float32)
        parts.append(p)
        rest = rest - p
    qc = np.zeros((C_HEADS, 1, LANES), np.float32)
    kc = np.zeros((C_HEADS, 1, LANES), np.float32)
    for a, p in enumerate(parts):
        qc[:, 0, C_AUG_POS + 2 * a] = p
        qc[:, 0, C_AUG_POS + 2 * a + 1] = p
        kc[:, 0, 2 * a] = p * (1 << C_POS_BITS)
        kc[:, 0, 2 * a + 1] = p
    return jnp.asarray(c), jnp.asarray(qc), jnp.asarray(kc)


def _attn_c_kernel(c_ref, qc_ref, kc_ref, q_ref, k_ref, v_ref, lam_ref, sg_ref, o_ref,
                   kaug_ref, qv_ref, bias_ref, u_ref, p_ref, m_ref, l_ref, alpha_ref, acc_ref,
                   *, t, seq, lambda_init):
    h = pl.program_id(1)
    qi = pl.program_id(2)
    nk = seq // t
    lane = lax.broadcasted_iota(jnp.int32, (1, LANES), 1)
    hi_lane = (lane & 1) == 0
    lo_mask = (1 << C_POS_BITS) - 1

    @pl.when(qi == 0)
    def _():
        key_pos_lane = (lane >= C_AUG_POS) & (lane < 2 * C_AUG_POS)

        def fill(kb, carry):
            rows = pl.ds(pl.multiple_of(kb * t, t), t)
            pos = kb * t + lax.broadcasted_iota(jnp.int32, (t, 1), 0)
            hi = (pos - (pos & lo_mask)).astype(F32)
            lo = (pos & lo_mask).astype(F32)
            aug = jnp.where(key_pos_lane, -jnp.where(hi_lane, hi, lo), kc_ref[0]).astype(BF16)
            for j in range(2):
                kaug_ref[j, rows, 0:HEAD_DIM] = k_ref[rows, j * HEAD_DIM:(j + 1) * HEAD_DIM]
                kaug_ref[j, rows, HEAD_DIM:2 * HEAD_DIM] = aug
            return carry

        lax.fori_loop(0, nk, fill, 0)
        ii = lax.broadcasted_iota(jnp.int32, (t, 1), 0)
        jj = lax.broadcasted_iota(jnp.int32, (1, t), 1)
        bias_ref[...] = c_ref[h] * jnp.abs(ii - jj).astype(F32)

    qpos = qi * t + lax.broadcasted_iota(jnp.int32, (t, 1), 0)
    q_hi = (qpos >> C_POS_BITS).astype(F32)
    q_lo = (qpos & lo_mask).astype(F32)
    q_aug = jnp.where(lane < C_AUG_POS, jnp.where(hi_lane, q_hi, q_lo), qc_ref[0])
    for j in range(2):
        qj = q_ref[:, j * HEAD_DIM:(j + 1) * HEAD_DIM]
        for variant, aug in enumerate((-q_aug, q_aug, jnp.zeros_like(q_aug))):
            qv_ref[variant, j, :, 0:HEAD_DIM] = qj
            qv_ref[variant, j, :, HEAD_DIM:2 * HEAD_DIM] = aug.astype(BF16)

    def key_rows(step):
        kb = qi + step
        kb = jnp.where(kb >= nk, kb - nk, kb)
        return kb, pl.ds(pl.multiple_of(kb * t, t), t)

    def scores(step, slot):
        kb, rows = key_rows(step)
        variant = 2 if step == 0 else jnp.where(kb < qi, 0, 1)
        for j in range(2):
            u_ref[slot, j] = _dot_nt(qv_ref[variant, j], kaug_ref[j, rows, :])

    def accumulate(step, slot):
        _, rows = key_rows(step)
        vblk = v_ref[rows, :]
        for j in range(2):
            for r0 in range(0, t, C_STRIP):
                rs = slice(r0, r0 + C_STRIP)
                u = u_ref[slot, j, rs, :]
                if step == 0:
                    u = u - bias_ref[rs, :]
                    m_new = jnp.broadcast_to(jnp.max(u, axis=1, keepdims=True), (C_STRIP, LANES))
                else:
                    m_prev = m_ref[j, rs, :]
                    m_new = jnp.maximum(m_prev, jnp.max(u, axis=1, keepdims=True))
                    alpha = jnp.exp2(m_prev - m_new)
                    alpha_ref[j, rs, :] = alpha
                p = jnp.exp2(u - jnp.tile(m_new, (1, t // LANES)))
                psum = p[:, 0:LANES]
                for cb in range(1, t // LANES):
                    psum = psum + p[:, cb * LANES:(cb + 1) * LANES]
                l_ref[j, rs, :] = psum if step == 0 else alpha * l_ref[j, rs, :] + psum
                m_ref[j, rs, :] = m_new
                p_ref[slot, j, rs, :] = p.astype(BF16)
            pv = _dot(p_ref[slot, j], vblk)
            acc_ref[j] = pv if step == 0 else jnp.tile(alpha_ref[j], (1, 2)) * acc_ref[j] + pv

    scores(0, 0)
    for step in range(nk):
        if step + 1 < nk:
            scores(step + 1, (step + 1) % 2)
        accumulate(step, step % 2)

    lp = lam_ref[...]
    lam = (jnp.exp(jnp.sum(lp[0:1] * lp[1:2], axis=-1, keepdims=True))
           - jnp.exp(jnp.sum(lp[2:3] * lp[3:4], axis=-1, keepdims=True)) + lambda_init)
    den0 = jnp.sum(l_ref[0], axis=1, keepdims=True)
    den1 = jnp.sum(l_ref[1], axis=1, keepdims=True)
    o = acc_ref[0] / den0 - lam * (acc_ref[1] / den1)
    o = (_rms(o) * sg_ref[...]) * (1.0 - lambda_init)
    o_ref[...] = o.astype(o_ref.dtype)


def _attn_c(proj, lam_params, subln_g, wl, lambda_init, batch, seq, t=512):
    wh = 2 * HEAD_DIM
    d = C_HEADS * wh
    t = min(t, seq)
    nq = seq // t
    assert seq <= (1 << (2 * C_POS_BITS)) and t % LANES == 0
    c, qc, kc = _attn_c_consts()
    const_spec = pl.BlockSpec((1, 1, LANES), lambda b, h, qi: (h, 0, 0))
    return pl.pallas_call(
        functools.partial(_attn_c_kernel, t=t, seq=seq, lambda_init=lambda_init),
        grid=(batch, C_HEADS, nq),
        in_specs=[
            pl.BlockSpec(memory_space=pltpu.SMEM),
            const_spec, const_spec,
            pl.BlockSpec((t, wh), lambda b, h, qi: (b * nq + qi, h)),
            pl.BlockSpec((seq, wh), lambda b, h, qi: (b, C_HEADS + h)),
            pl.BlockSpec((seq, wh), lambda b, h, qi: (b, 2 * C_HEADS + h)),
            pl.BlockSpec((None, 4, HEAD_DIM), lambda b, h, qi: (wl, 0, 0)),
            pl.BlockSpec((None, 1, wh), lambda b, h, qi: (wl, 0, 0)),
        ],
        out_specs=pl.BlockSpec((t, wh), lambda b, h, qi: (b * nq + qi, h)),
        out_shape=jax.ShapeDtypeStruct((batch * seq, d), BF16),
        scratch_shapes=[
            pltpu.VMEM((2, seq, wh), BF16),
            pltpu.VMEM((3, 2, t, wh), BF16),
            pltpu.VMEM((t, t), F32),
            pltpu.VMEM((2, 2, t, t), F32),
            pltpu.VMEM((2, 2, t, t), BF16),
            pltpu.VMEM((2, t, LANES), F32),
            pltpu.VMEM((2, t, LANES), F32),
            pltpu.VMEM((2, t, LANES), F32),
            pltpu.VMEM((2, t, wh), F32),
        ],
        compiler_params=pltpu.CompilerParams(
            dimension_semantics=("parallel", "parallel", "arbitrary"),
            vmem_limit_bytes=VMEM_LIMIT_BYTES),
        name="attn_c",
    )(c, qc, kc, proj, proj, proj, lam_params, subln_g.reshape(-1, 1, wh))


def kernel(x, c, ada_w, ada_b, norm_g, ab_w_in, ab_w_out, a_sink, c_w_in, c_w_out, c_lambda,
           c_subln_g, ffn_w_up, ffn_conv_w, ffn_conv_b, ffn_w_down):
    batch, seq, d = x.shape
    depth = ada_w.shape[0]
    dff = ffn_w_down.shape[1]
    dff_pad = -(-dff // FFN_TF) * FFN_TF

    ab_w_in_b, ab_w_out_b = ab_w_in.astype(BF16), ab_w_out.astype(BF16)
    c_w_in_b, c_w_out_b = c_w_in.astype(BF16), c_w_out.astype(BF16)
    w_gate_up = _prep_w_up(ffn_w_up, dff_pad)
    w_down = _prep_w_down(ffn_w_down, dff_pad)
    conv_w = jnp.pad(ffn_conv_w, ((0, 0), (0, 0), (0, dff_pad - dff)))
    conv_b = jnp.pad(ffn_conv_b, ((0, 0), (0, dff_pad - dff))).reshape(depth, 1, dff_pad)

    q_scale = np.float32(HEAD_DIM ** -0.5 * math.log2(math.e))
    ab_col_scale = jnp.ones((ab_w_in.shape[2],), F32)
    c_col_scale = jnp.asarray(np.concatenate([np.full((d,), q_scale, np.float32),
                                              np.ones((2 * d,), np.float32)]))

    mod = _ada(c, ada_w, ada_b)
    xs = x.reshape(batch * seq, d)
    for layer in range(depth):
        wl = layer // 2
        if layer % 2 == 0:
            proj = _norm_proj(xs, norm_g, mod, layer, ab_w_in_b, wl, ab_col_scale, seq)
            mixed = _attn_ab(proj, a_sink, wl, batch, seq)
            xs = _out_proj(mixed, ab_w_out_b, wl, xs, norm_g, mod, layer, seq)
        else:
            lambda_init = 0.8 - 0.6 * math.exp(-0.3 * layer)
            proj = _norm_proj(xs, norm_g, mod, layer, c_w_in_b, wl, c_col_scale, seq)
            mixed = _attn_c(proj, c_lambda, c_subln_g, wl, lambda_init, batch, seq)
            xs = _out_proj(mixed, c_w_out_b, wl, xs, norm_g, mod, layer, seq)
        xs = _ffn(xs, norm_g, mod, layer, w_gate_up, conv_w, conv_b, w_down, seq)
    return xs.reshape(batch, seq, d)
```

```python
import functools
import math

import numpy as np
import jax
import jax.numpy as jnp
from jax import lax
from jax.experimental import pallas as pl
from jax.experimental.pallas import tpu as pltpu

F32 = jnp.float32
BF16 = jnp.bfloat16

HEAD_DIM = 128
A_Q_HEADS = 4
A_KV_HEADS = 2
A_GROUP = A_Q_HEADS // A_KV_HEADS
A_RADIUS = 128
B_PAIRS = ((128, 1), (512, 4), (2048, 16))
B_HEADS = 4
N_MIX_HEADS = A_Q_HEADS + len(B_PAIRS) * B_HEADS
C_HEADS = 8
CONV_WIDTH = 3
EPS = 1e-6
NEG_INF = -1e30

SUBLANES_F32 = 8
SUBLANES_BF16 = 16
LANES = 128
VMEM_LIMIT_BYTES = 56 * 1024 * 1024

AB_BLOCK_Q = 128
FFN_HALO = SUBLANES_BF16
FFN_TF = 512
AB_PROJ_TN = 1792
C_PROJ_TN = 1536


def _alibi_slopes(n):
    return (2.0 ** (-8.0 * np.arange(1, n + 1, dtype=np.float32) / n)).astype(np.float32)


def _rms(x):
    return x * lax.rsqrt(jnp.mean(x * x, axis=-1, keepdims=True) + EPS)


def _dot(a, b):
    return jnp.dot(a, b, preferred_element_type=F32)


def _dot_nt(a, b):
    return lax.dot_general(a, b, (((1,), (1,)), ((), ())), preferred_element_type=F32)


def _ada_kernel(c_ref, w_ref, b_ref, o_ref):
    batch = c_ref.shape[0]
    tn = w_ref.shape[1]
    for b in range(batch):
        cb = c_ref[b]
        cond = cb * jax.nn.sigmoid(cb)
        prod = w_ref[...] * jnp.tile(cond, (1, tn // LANES))
        o_ref[b:b + 1, :] = jnp.sum(prod, axis=0, keepdims=True) + b_ref[...]
    o_ref[batch:, :] = jnp.zeros((o_ref.shape[0] - batch, tn), F32)


def _ada(c, ada_w, ada_b, tn=1024):
    depth, d, n = ada_w.shape
    b = c.shape[0]
    assert b < SUBLANES_F32 and n % d == 0 and d % tn == 0
    per_chunk = d // tn
    c_lanes = jnp.broadcast_to(c[:, :, None], (b, d, LANES))
    return pl.pallas_call(
        _ada_kernel,
        grid=(depth, n // tn),
        in_specs=[
            pl.BlockSpec((b, d, LANES), lambda l, j: (0, 0, 0)),
            pl.BlockSpec((None, d, tn), lambda l, j: (l, 0, j)),
            pl.BlockSpec((None, 1, tn), lambda l, j: (l, 0, j)),
        ],
        out_specs=pl.BlockSpec((None, None, SUBLANES_F32, tn),
                               lambda l, j: (l, j // per_chunk, 0, j % per_chunk)),
        out_shape=jax.ShapeDtypeStruct((depth, n // d, SUBLANES_F32, d), F32),
        compiler_params=pltpu.CompilerParams(
            dimension_semantics=("arbitrary", "arbitrary"), vmem_limit_bytes=VMEM_LIMIT_BYTES),
        name="ada",
    )(c_lanes, ada_w, ada_b.reshape(depth, 1, n))


MOD_SH1, MOD_SC1, MOD_G1, MOD_SH2, MOD_SC2, MOD_G2 = range(6)
NORM_PRE_MIX, NORM_POST_MIX, NORM_PRE_FFN, NORM_POST_FFN = range(4)


def _mod_spec(layer, chunk, d):
    return pl.BlockSpec((None, None, SUBLANES_F32, d), lambda *_: (layer, chunk, 0, 0))


def _norm_spec(layer, d):
    return pl.BlockSpec((None, 4, d), lambda *_: (layer, 0, 0))


def _batch_row(ref, b):
    return ref[pl.ds(b, 1), :]


NORM_STRIP = 32
NORM_UNROLL = 8


def _strip_rows(s):
    return pl.ds(pl.multiple_of(s * NORM_STRIP, NORM_STRIP), NORM_STRIP)


def _row_rsqrt_ms(src_ref, rs_ref):
    n_rows, d = src_ref.shape

    def body(s, carry):
        rows = _strip_rows(s)
        x = src_ref[rows, :]
        ms = jnp.sum(x * x, axis=1, keepdims=True) * (1.0 / d)
        rs_ref[rows, :] = jnp.broadcast_to(lax.rsqrt(ms + EPS), (NORM_STRIP, LANES))
        return carry

    lax.fori_loop(0, n_rows // NORM_STRIP, body, 0, unroll=NORM_UNROLL)


def _scaled_rows(src_ref, rs_ref, rows):
    return src_ref[rows, :] * jnp.tile(rs_ref[rows, :], (1, src_ref.shape[1] // LANES))


def _cast_pad_kernel(w_ref, o_ref, *, axis):
    n = w_ref.shape[axis]
    if axis == 0:
        o_ref[:n, :] = w_ref[...].astype(o_ref.dtype)
        o_ref[n:, :] = jnp.zeros((o_ref.shape[0] - n, o_ref.shape[1]), o_ref.dtype)
    else:
        o_ref[:, :n] = w_ref[...].astype(o_ref.dtype)
        o_ref[:, n:] = jnp.zeros((o_ref.shape[0], o_ref.shape[1] - n), o_ref.dtype)


def _prep_w_up(w_up, dff_pad, tr=256):
    depth, d, two_dff = w_up.shape
    dff = two_dff // 2
    return pl.pallas_call(
        functools.partial(_cast_pad_kernel, axis=1),
        grid=(depth, 2, d // tr),
        in_specs=[pl.BlockSpec((None, tr, dff), lambda l, p, r: (l, r, p))],
        out_specs=pl.BlockSpec((None, None, tr, dff_pad), lambda l, p, r: (l, p, r, 0)),
        out_shape=jax.ShapeDtypeStruct((depth, 2, d, dff_pad), BF16),
        compiler_params=pltpu.CompilerParams(
            dimension_semantics=("arbitrary",) * 3, vmem_limit_bytes=VMEM_LIMIT_BYTES),
        name="prep_w_up",
    )(w_up)


def _prep_w_down(w_down, dff_pad, tc=256):
    depth, dff, d = w_down.shape
    return pl.pallas_call(
        functools.partial(_cast_pad_kernel, axis=0),
        grid=(depth, d // tc),
        in_specs=[pl.BlockSpec((None, dff, tc), lambda l, c: (l, 0, c))],
        out_specs=pl.BlockSpec((None, dff_pad, tc), lambda l, c: (l, 0, c)),
        out_shape=jax.ShapeDtypeStruct((depth, dff_pad, d), BF16),
        compiler_params=pltpu.CompilerParams(
            dimension_semantics=("arbitrary",) * 2, vmem_limit_bytes=VMEM_LIMIT_BYTES),
        name="prep_w_down",
    )(w_down)


def _norm_proj_kernel(x_ref, g_ref, sc_ref, sh_ref, w_ref, cs_ref, o_ref, h_ref, rs_ref, *, per_seq):
    @pl.when(pl.program_id(1) == 0)
    def _():
        b = pl.program_id(0) // per_seq
        g_mod = g_ref[NORM_PRE_MIX:NORM_PRE_MIX + 1, :] * (1.0 + _batch_row(sc_ref, b))
        shift = _batch_row(sh_ref, b)
        _row_rsqrt_ms(x_ref, rs_ref)

        def body(s, carry):
            rows = _strip_rows(s)
            h_ref[rows, :] = (_scaled_rows(x_ref, rs_ref, rows) * g_mod + shift).astype(BF16)
            return carry

        lax.fori_loop(0, x_ref.shape[0] // NORM_STRIP, body, 0, unroll=NORM_UNROLL)

    o_ref[...] = (_dot(h_ref[...], w_ref[...]) * cs_ref[...]).astype(o_ref.dtype)


def _norm_proj(x, norm_g, mod, layer, w, wl, col_scale, seq, tm=1024, tn=512):
    m, d = x.shape
    n = w.shape[2]
    tm = min(tm, seq)
    return pl.pallas_call(
        functools.partial(_norm_proj_kernel, per_seq=seq // tm),
        grid=(m // tm, n // tn),
        in_specs=[
            pl.BlockSpec((tm, d), lambda i, j: (i, 0)),
            _norm_spec(layer, d),
            _mod_spec(layer, MOD_SC1, d),
            _mod_spec(layer, MOD_SH1, d),
            pl.BlockSpec((None, d, tn), lambda i, j: (wl, 0, j)),
            pl.BlockSpec((1, tn), lambda i, j: (0, j)),
        ],
        out_specs=pl.BlockSpec((tm, tn), lambda i, j: (i, j)),
        out_shape=jax.ShapeDtypeStruct((m, n), BF16),
        scratch_shapes=[pltpu.VMEM((tm, d), BF16), pltpu.VMEM((tm, LANES), F32)],
        compiler_params=pltpu.CompilerParams(
            dimension_semantics=("parallel", "arbitrary"), vmem_limit_bytes=VMEM_LIMIT_BYTES),
        name="norm_proj",
    )(x, norm_g, mod, mod, w, col_scale.reshape(1, n))


def _out_proj_kernel(a_ref, w_ref, x_ref, g_ref, gate_ref, o_ref, *, per_seq):
    b = pl.program_id(0) // per_seq
    y = _dot(a_ref[...], w_ref[...])
    g = g_ref[NORM_POST_MIX:NORM_POST_MIX + 1, :]
    o_ref[...] = x_ref[...] + _batch_row(gate_ref, b) * (_rms(y) * g)


def _out_proj(a, w, wl, x, norm_g, mod, layer, seq, tm=512):
    m, k = a.shape
    d = w.shape[2]
    tm = min(tm, seq)
    return pl.pallas_call(
        functools.partial(_out_proj_kernel, per_seq=seq // tm),
        grid=(m // tm,),
        in_specs=[
            pl.BlockSpec((tm, k), lambda i: (i, 0)),
            pl.BlockSpec((None, k, d), lambda i: (wl, 0, 0)),
            pl.BlockSpec((tm, d), lambda i: (i, 0)),
            _norm_spec(layer, d),
            _mod_spec(layer, MOD_G1, d),
        ],
        out_specs=pl.BlockSpec((tm, d), lambda i: (i, 0)),
        out_shape=jax.ShapeDtypeStruct((m, d), F32),
        compiler_params=pltpu.CompilerParams(
            dimension_semantics=("parallel",), vmem_limit_bytes=VMEM_LIMIT_BYTES),
        name="out_proj",
    )(a, w, x, norm_g, mod)


def _ffn_kernel(x_ref, xp_ref, xn_ref, g_ref, sc_ref, sh_ref, wg_ref, wu_ref, cw_ref, cb_ref,
                wd_ref, gate_ref, o_ref, h_ref, rs_ref, *, tm, per_seq):
    i = pl.program_id(0)
    f = pl.program_id(1)
    b = i // per_seq
    halo = FFN_HALO
    d = x_ref.shape[1]

    @pl.when(f == 0)
    def _():
        g_mod = g_ref[NORM_PRE_FFN:NORM_PRE_FFN + 1, :] * (1.0 + _batch_row(sc_ref, b))
        shift = _batch_row(sh_ref, b)

        keep_prev = jnp.where(i % per_seq == 0, 0.0, 1.0)
        keep_next = jnp.where(i % per_seq == per_seq - 1, 0.0, 1.0)
        zeros = jnp.zeros((halo - SUBLANES_F32, d), F32)
        hp = (_rms(xp_ref[...]) * g_mod + shift) * keep_prev
        hn = (_rms(xn_ref[...]) * g_mod + shift) * keep_next
        h_ref[0:halo, :] = jnp.concatenate([zeros, hp], axis=0).astype(BF16)
        h_ref[halo + tm:, :] = jnp.concatenate([hn, zeros], axis=0).astype(BF16)

        _row_rsqrt_ms(x_ref, rs_ref)

        def body(s, carry):
            rows = _strip_rows(s)
            dst = pl.ds(pl.multiple_of(halo + s * NORM_STRIP, SUBLANES_BF16), NORM_STRIP)
            h_ref[dst, :] = (_scaled_rows(x_ref, rs_ref, rows) * g_mod + shift).astype(BF16)
            return carry

        lax.fori_loop(0, tm // NORM_STRIP, body, 0, unroll=NORM_UNROLL)
        o_ref[...] = jnp.zeros_like(o_ref)

    rows = tm + 2 * halo
    gate = _dot(h_ref[...], wg_ref[...])
    up = _dot(h_ref[halo:halo + tm, :], wu_ref[...])
    g_prev = pltpu.roll(gate, 1, 0)[halo:halo + tm]
    g_next = pltpu.roll(gate, rows - 1, 0)[halo:halo + tm]
    g_mid = gate[halo:halo + tm]
    cw = cw_ref[...]
    z = g_prev * cw[0:1] + g_mid * cw[1:2] + g_next * cw[2:3] + cb_ref[...]
    act = jax.nn.gelu(z, approximate=True) * up
    o_ref[...] += _dot(act.astype(BF16), wd_ref[...])

    @pl.when(f == pl.num_programs(1) - 1)
    def _():
        g_gate = g_ref[NORM_POST_FFN:NORM_POST_FFN + 1, :] * _batch_row(gate_ref, b)
        _row_rsqrt_ms(o_ref, rs_ref)

        def body(s, carry):
            rows = _strip_rows(s)
            o_ref[rows, :] = x_ref[rows, :] + _scaled_rows(o_ref, rs_ref, rows) * g_gate
            return carry

        lax.fori_loop(0, tm // NORM_STRIP, body, 0, unroll=NORM_UNROLL)


def _ffn(x, norm_g, mod, layer, w_gate_up, conv_w, conv_b, w_down, seq, tm=1024, tf=FFN_TF):
    m, d = x.shape
    dff = w_down.shape[1]
    tm = min(tm, seq)
    per_seq = seq // tm
    nblk8 = m // SUBLANES_F32
    r8 = tm // SUBLANES_F32
    return pl.pallas_call(
        functools.partial(_ffn_kernel, tm=tm, per_seq=per_seq),
        grid=(m // tm, dff // tf),
        in_specs=[
            pl.BlockSpec((tm, d), lambda i, f: (i, 0)),
            pl.BlockSpec((SUBLANES_F32, d), lambda i, f: (jnp.maximum(i * r8 - 1, 0), 0)),
            pl.BlockSpec((SUBLANES_F32, d), lambda i, f: (jnp.minimum((i + 1) * r8, nblk8 - 1), 0)),
            _norm_spec(layer, d),
            _mod_spec(layer, MOD_SC2, d),
            _mod_spec(layer, MOD_SH2, d),
            pl.BlockSpec((None, None, d, tf), lambda i, f: (layer, 0, 0, f)),
            pl.BlockSpec((None, None, d, tf), lambda i, f: (layer, 1, 0, f)),
            pl.BlockSpec((None, CONV_WIDTH, tf), lambda i, f: (layer, 0, f)),
            pl.BlockSpec((None, 1, tf), lambda i, f: (layer, 0, f)),
            pl.BlockSpec((None, tf, d), lambda i, f: (layer, f, 0)),
            _mod_spec(layer, MOD_G2, d),
        ],
        out_specs=pl.BlockSpec((tm, d), lambda i, f: (i, 0), pipeline_mode=pl.Buffered(1)),
        out_shape=jax.ShapeDtypeStruct((m, d), F32),
        scratch_shapes=[pltpu.VMEM((tm + 2 * FFN_HALO, d), BF16), pltpu.VMEM((tm, LANES), F32)],
        compiler_params=pltpu.CompilerParams(
            dimension_semantics=("parallel", "arbitrary"), vmem_limit_bytes=VMEM_LIMIT_BYTES),
        name="ffn",
    )(x, x, x, norm_g, mod, mod, w_gate_up, w_gate_up, conv_w, conv_b, w_down, mod)


def _attn_ab_kernel(slopes_ref, sink_ref, qa_ref, qb0_ref, qb1_ref, qb2_ref, ka_ref, va_ref,
                    kb_ref, vb_ref, o_ref, *, seq, wl):
    tq = AB_BLOCK_Q
    t0 = pl.program_id(1) * tq
    scale = HEAD_DIM ** -0.5
    qpos = t0 + lax.broadcasted_iota(jnp.int32, (tq, 1), 0)

    def window(span):
        width = min(tq + 2 * span, seq)
        start = jnp.clip(t0 - span, 0, seq - width)
        start = pl.multiple_of(start, SUBLANES_BF16)
        kpos = start + lax.broadcasted_iota(jnp.int32, (1, width), 1)
        dist = jnp.abs(qpos - kpos)
        return start, width, dist

    def head(ref, h):
        return ref[:, h * HEAD_DIM:(h + 1) * HEAD_DIM]

    start, width, dist = window(A_RADIUS)
    valid = dist <= A_RADIUS
    distf = dist.astype(F32)
    for kvh in range(A_KV_HEADS):
        cols = slice(kvh * HEAD_DIM, (kvh + 1) * HEAD_DIM)
        kw = ka_ref[pl.ds(start, width), cols]
        vw = va_ref[pl.ds(start, width), cols]
        for gq in range(A_GROUP):
            hq = kvh * A_GROUP + gq
            s = _dot_nt(head(qa_ref, hq), kw) * scale
            s = jnp.where(valid, s - slopes_ref[hq] * distf, NEG_INF)
            sink = sink_ref[wl, hq]
            m = jnp.maximum(s.max(axis=-1, keepdims=True), sink)
            p = jnp.exp(s - m)
            den = p.sum(axis=-1, keepdims=True) + jnp.exp(sink - m)
            o = _dot(p.astype(BF16), vw) / den
            o_ref[:, hq * HEAD_DIM:(hq + 1) * HEAD_DIM] = o.astype(o_ref.dtype)

    qb_refs = (qb0_ref, qb1_ref, qb2_ref)
    geom = []
    for window_len, dil in B_PAIRS:
        span = (window_len // (2 * dil)) * dil
        start, width, dist = window(span)
        valid = (dist <= span) & ((dist & (dil - 1)) == 0)
        geom.append((start, width, valid, dist.astype(F32)))
    for hb in range(B_HEADS):
        cols = slice(hb * HEAD_DIM, (hb + 1) * HEAD_DIM)
        scores, vwins = [], []
        for gi in range(len(B_PAIRS)):
            start, width, valid, distf = geom[gi]
            kw = kb_ref[pl.ds(start, width), cols]
            vwins.append(vb_ref[pl.ds(start, width), cols])
            slope = slopes_ref[A_Q_HEADS + gi * B_HEADS + hb]
            s = _dot_nt(head(qb_refs[gi], hb), kw) * scale
            scores.append(jnp.where(valid, s - slope * distf, NEG_INF))
        m = functools.reduce(jnp.maximum, [s.max(axis=-1, keepdims=True) for s in scores])
        den = jnp.zeros((tq, 1), F32)
        acc = jnp.zeros((tq, HEAD_DIM), F32)
        for s, vw in zip(scores, vwins):
            p = jnp.exp(s - m)
            den = den + p.sum(axis=-1, keepdims=True)
            acc = acc + _dot(p.astype(BF16), vw)
        col = (A_Q_HEADS + hb) * HEAD_DIM
        o_ref[:, col:col + HEAD_DIM] = (acc / den).astype(o_ref.dtype)


def _attn_ab(proj, sink, wl, batch, seq):
    tq = min(AB_BLOCK_Q, seq)
    nq = seq // tq
    wq = A_Q_HEADS * HEAD_DIM
    wka = A_KV_HEADS * HEAD_DIM
    wkb = B_HEADS * HEAD_DIM
    off_ka = wq
    off_qb = wq + 2 * wka
    off_kb = off_qb + len(B_PAIRS) * wkb
    assert all(d & (d - 1) == 0 for _, d in B_PAIRS)
    assert off_ka % wka == 0 and off_qb % wq == 0 and off_kb % wkb == 0
    slopes = jnp.asarray(_alibi_slopes(N_MIX_HEADS))
    smem = pl.BlockSpec(memory_space=pltpu.SMEM)

    def qspec(col_block):
        return pl.BlockSpec((tq, wq), lambda b, n: (b * nq + n, col_block))

    def kvspec(width, col_block):
        return pl.BlockSpec((seq, width), lambda b, n: (b, col_block))

    return pl.pallas_call(
        functools.partial(_attn_ab_kernel, seq=seq, wl=wl),
        grid=(batch, nq),
        in_specs=[
            smem, smem,
            qspec(0), qspec(off_qb // wq), qspec(off_qb // wq + 1), qspec(off_qb // wq + 2),
            kvspec(wka, off_ka // wka), kvspec(wka, off_ka // wka + 1),
            kvspec(wkb, off_kb // wkb), kvspec(wkb, off_kb // wkb + 1),
        ],
        out_specs=pl.BlockSpec((tq, wq + wkb), lambda b, n: (b * nq + n, 0)),
        out_shape=jax.ShapeDtypeStruct((batch * seq, wq + wkb), BF16),
        compiler_params=pltpu.CompilerParams(
            dimension_semantics=("parallel", "arbitrary"), vmem_limit_bytes=VMEM_LIMIT_BYTES),
        name="attn_ab",
    )(slopes, sink, proj, proj, proj, proj, proj, proj, proj, proj)


C_POS_BITS = 6
C_SLOPE_PARTS = 3
C_AUG_POS = 2 * C_SLOPE_PARTS
C_STRIP = 32


def _attn_c_consts():
    c = (_alibi_slopes(C_HEADS).astype(np.float64) * math.log2(math.e)).astype(np.float32)
    parts, rest = [], c.copy()
    for _ in range(C_SLOPE_PARTS):
        p = rest.astype(BF16).astype(np.float32)
        parts.append(p)
        rest = rest - p
    qc = np.zeros((C_HEADS, 1, LANES), np.float32)
    kc = np.zeros((C_HEADS, 1, LANES), np.float32)
    for a, p in enumerate(parts):
        qc[:, 0, C_AUG_POS + 2 * a] = p
        qc[:, 0, C_AUG_POS + 2 * a + 1] = p
        kc[:, 0, 2 * a] = p * (1 << C_POS_BITS)
        kc[:, 0, 2 * a + 1] = p
    return jnp.asarray(c), jnp.asarray(qc), jnp.asarray(kc)


def _attn_c_kernel(c_ref, qc_ref, kc_ref, q_ref, k_ref, v_ref, lam_ref, sg_ref, o_ref,
                   kaug_ref, qv_ref, bias_ref, u_ref, p_ref, m_ref, l_ref, alpha_ref, acc_ref,
                   *, tq, tk, seq, lambda_init):
    h = pl.program_id(1)
    qi = pl.program_id(2)
    nk = seq // tk
    ratio = tk // tq
    kd = qi // ratio
    off = qi - kd * ratio
    lane = lax.broadcasted_iota(jnp.int32, (1, LANES), 1)
    hi_lane = (lane & 1) == 0
    lo_mask = (1 << C_POS_BITS) - 1

    @pl.when(qi == 0)
    def _():
        key_pos_lane = (lane >= C_AUG_POS) & (lane < 2 * C_AUG_POS)

        def fill(kb, carry):
            rows = pl.ds(pl.multiple_of(kb * tk, tk), tk)
            pos = kb * tk + lax.broadcasted_iota(jnp.int32, (tk, 1), 0)
            hi = (pos - (pos & lo_mask)).astype(F32)
            lo = (pos & lo_mask).astype(F32)
            aug = jnp.where(key_pos_lane, -jnp.where(hi_lane, hi, lo), kc_ref[0]).astype(BF16)
            for j in range(2):
                kaug_ref[j, rows, 0:HEAD_DIM] = k_ref[rows, j * HEAD_DIM:(j + 1) * HEAD_DIM]
                kaug_ref[j, rows, HEAD_DIM:2 * HEAD_DIM] = aug
            return carry

        lax.fori_loop(0, nk, fill, 0)
        jj = lax.broadcasted_iota(jnp.int32, (1, tk), 1)
        for o in range(ratio):
            ii = o * tq + lax.broadcasted_iota(jnp.int32, (tq, 1), 0)
            bias_ref[o] = c_ref[h] * jnp.abs(ii - jj).astype(F32)

    qpos = qi * tq + lax.broadcasted_iota(jnp.int32, (tq, 1), 0)
    q_hi = (qpos >> C_POS_BITS).astype(F32)
    q_lo = (qpos & lo_mask).astype(F32)
    q_aug = jnp.where(lane < C_AUG_POS, jnp.where(hi_lane, q_hi, q_lo), qc_ref[0])
    for j in range(2):
        qj = q_ref[:, j * HEAD_DIM:(j + 1) * HEAD_DIM]
        for variant, aug in enumerate((-q_aug, q_aug, jnp.zeros_like(q_aug))):
            qv_ref[variant, j, :, 0:HEAD_DIM] = qj
            qv_ref[variant, j, :, HEAD_DIM:2 * HEAD_DIM] = aug.astype(BF16)

    def key_rows(step):
        kb = kd + step
        kb = jnp.where(kb >= nk, kb - nk, kb)
        return kb, pl.ds(pl.multiple_of(kb * tk, tk), tk)

    def scores(step, slot):
        kb, rows = key_rows(step)
        variant = 2 if step == 0 else jnp.where(kb < kd, 0, 1)
        for j in range(2):
            u_ref[slot, j] = _dot_nt(qv_ref[variant, j], kaug_ref[j, rows, :])

    def accumulate(step, slot):
        _, rows = key_rows(step)
        vblk = v_ref[rows, :]
        for j in range(2):
            for r0 in range(0, tq, C_STRIP):
                rs = slice(r0, r0 + C_STRIP)
                u = u_ref[slot, j, rs, :]
                if step == 0:
                    u = u - bias_ref[off, rs, :]
                    m_new = jnp.broadcast_to(jnp.max(u, axis=1, keepdims=True), (C_STRIP, LANES))
                else:
                    m_prev = m_ref[j, rs, :]
                    m_new = jnp.maximum(m_prev, jnp.max(u, axis=1, keepdims=True))
                    alpha = jnp.exp2(m_prev - m_new)
                    alpha_ref[j, rs, :] = alpha
                p = jnp.exp2(u - jnp.tile(m_new, (1, tk // LANES)))
                psum = p[:, 0:LANES]
                for cb in range(1, tk // LANES):
                    psum = psum + p[:, cb * LANES:(cb + 1) * LANES]
                l_ref[j, rs, :] = psum if step == 0 else alpha * l_ref[j, rs, :] + psum
                m_ref[j, rs, :] = m_new
                p_ref[slot, j, rs, :] = p.astype(BF16)
            pv = _dot(p_ref[slot, j], vblk)
            acc_ref[j] = pv if step == 0 else jnp.tile(alpha_ref[j], (1, 2)) * acc_ref[j] + pv

    scores(0, 0)
    for step in range(nk):
        if step + 1 < nk:
            scores(step + 1, (step + 1) % 2)
        accumulate(step, step % 2)

    lp = lam_ref[...]
    lam = (jnp.exp(jnp.sum(lp[0:1] * lp[1:2], axis=-1, keepdims=True))
           - jnp.exp(jnp.sum(lp[2:3] * lp[3:4], axis=-1, keepdims=True)) + lambda_init)
    den0 = jnp.sum(l_ref[0], axis=1, keepdims=True)
    den1 = jnp.sum(l_ref[1], axis=1, keepdims=True)
    o = acc_ref[0] / den0 - lam * (acc_ref[1] / den1)
    o = (_rms(o) * sg_ref[...]) * (1.0 - lambda_init)
    o_ref[...] = o.astype(o_ref.dtype)


def _attn_c(proj, lam_params, subln_g, wl, lambda_init, batch, seq, tq=512, tk=1024):
    wh = 2 * HEAD_DIM
    d = C_HEADS * wh
    tk = min(tk, seq)
    tq = min(tq, tk)
    nq = seq // tq
    assert seq <= (1 << (2 * C_POS_BITS)) and tk % tq == 0 and tq % C_STRIP == 0 and tk % LANES == 0
    c, qc, kc = _attn_c_consts()
    const_spec = pl.BlockSpec((1, 1, LANES), lambda b, h, qi: (h, 0, 0))
    return pl.pallas_call(
        functools.partial(_attn_c_kernel, tq=tq, tk=tk, seq=seq, lambda_init=lambda_init),
        grid=(batch, C_HEADS, nq),
        in_specs=[
            pl.BlockSpec(memory_space=pltpu.SMEM),
            const_spec, const_spec,
            pl.BlockSpec((tq, wh), lambda b, h, qi: (b * nq + qi, h)),
            pl.BlockSpec((seq, wh), lambda b, h, qi: (b, C_HEADS + h)),
            pl.BlockSpec((seq, wh), lambda b, h, qi: (b, 2 * C_HEADS + h)),
            pl.BlockSpec((None, 4, HEAD_DIM), lambda b, h, qi: (wl, 0, 0)),
            pl.BlockSpec((None, 1, wh), lambda b, h, qi: (wl, 0, 0)),
        ],
        out_specs=pl.BlockSpec((tq, wh), lambda b, h, qi: (b * nq + qi, h)),
        out_shape=jax.ShapeDtypeStruct((batch * seq, d), BF16),
        scratch_shapes=[
            pltpu.VMEM((2, seq, wh), BF16),
            pltpu.VMEM((3, 2, tq, wh), BF16),
            pltpu.VMEM((tk // tq, tq, tk), F32),
            pltpu.VMEM((2, 2, tq, tk), F32),
            pltpu.VMEM((2, 2, tq, tk), BF16),
            pltpu.VMEM((2, tq, LANES), F32),
            pltpu.VMEM((2, tq, LANES), F32),
            pltpu.VMEM((2, tq, LANES), F32),
            pltpu.VMEM((2, tq, wh), F32),
        ],
        compiler_params=pltpu.CompilerParams(
            dimension_semantics=("parallel", "parallel", "arbitrary"),
            vmem_limit_bytes=VMEM_LIMIT_BYTES),
        name="attn_c",
    )(c, qc, kc, proj, proj, proj, lam_params, subln_g.reshape(-1, 1, wh))


def kernel(x, c, ada_w, ada_b, norm_g, ab_w_in, ab_w_out, a_sink, c_w_in, c_w_out, c_lambda,
           c_subln_g, ffn_w_up, ffn_conv_w, ffn_conv_b, ffn_w_down):
    batch, seq, d = x.shape
    depth = ada_w.shape[0]
    dff = ffn_w_down.shape[1]
    dff_pad = -(-dff // FFN_TF) * FFN_TF

    ab_w_in_b, ab_w_out_b = ab_w_in.astype(BF16), ab_w_out.astype(BF16)
    c_w_in_b, c_w_out_b = c_w_in.astype(BF16), c_w_out.astype(BF16)
    w_gate_up = _prep_w_up(ffn_w_up, dff_pad)
    w_down = _prep_w_down(ffn_w_down, dff_pad)
    conv_w = jnp.pad(ffn_conv_w, ((0, 0), (0, 0), (0, dff_pad - dff)))
    conv_b = jnp.pad(ffn_conv_b, ((0, 0), (0, dff_pad - dff))).reshape(depth, 1, dff_pad)

    q_scale = np.float32(HEAD_DIM ** -0.5 * math.log2(math.e))
    ab_col_scale = jnp.ones((ab_w_in.shape[2],), F32)
    c_col_scale = jnp.asarray(np.concatenate([np.full((d,), q_scale, np.float32),
                                              np.ones((2 * d,), np.float32)]))

    mod = _ada(c, ada_w, ada_b)
    xs = x.reshape(batch * seq, d)
    for layer in range(depth):
        wl = layer // 2
        if layer % 2 == 0:
            proj = _norm_proj(xs, norm_g, mod, layer, ab_w_in_b, wl, ab_col_scale, seq, tn=AB_PROJ_TN)
            mixed = _attn_ab(proj, a_sink, wl, batch, seq)
            xs = _out_proj(mixed, ab_w_out_b, wl, xs, norm_g, mod, layer, seq)
        else:
            lambda_init = 0.8 - 0.6 * math.exp(-0.3 * layer)
            proj = _norm_proj(xs, norm_g, mod, layer, c_w_in_b, wl, c_col_scale, seq, tn=C_PROJ_TN)
            mixed = _attn_c(proj, c_lambda, c_subln_g, wl, lambda_init, batch, seq)
            xs = _out_proj(mixed, c_w_out_b, wl, xs, norm_g, mod, layer, seq)
        xs = _ffn(xs, norm_g, mod, layer, w_gate_up, conv_w, conv_b, w_down, seq)
    return xs.reshape(batch, seq, d)
```

```python
import functools
import math

import numpy as np
import jax
import jax.numpy as jnp
from jax import lax
from jax.experimental import pallas as pl
from jax.experimental.pallas import tpu as pltpu

F32 = jnp.float32
BF16 = jnp.bfloat16

HEAD_DIM = 128
A_Q_HEADS = 4
A_KV_HEADS = 2
A_GROUP = A_Q_HEADS // A_KV_HEADS
A_RADIUS = 128
B_PAIRS = ((128, 1), (512, 4), (2048, 16))
B_HEADS = 4
N_MIX_HEADS = A_Q_HEADS + len(B_PAIRS) * B_HEADS
C_HEADS = 8
CONV_WIDTH = 3
EPS = 1e-6
NEG_INF = -1e30

SUBLANES_F32 = 8
SUBLANES_BF16 = 16
LANES = 128
VMEM_LIMIT_BYTES = 56 * 1024 * 1024

AB_BLOCK_Q = 256
FFN_HALO = SUBLANES_BF16
FFN_TF = 512
AB_PROJ_TN = 512
AB_SLAB_FROM = 2
C_PROJ_TN = 1536
LOG2E = math.log2(math.e)


def _alibi_slopes(n):
    return (2.0 ** (-8.0 * np.arange(1, n + 1, dtype=np.float32) / n)).astype(np.float32)


def _rms(x):
    return x * lax.rsqrt(jnp.mean(x * x, axis=-1, keepdims=True) + EPS)


def _dot(a, b):
    return jnp.dot(a, b, preferred_element_type=F32)


def _dot_nt(a, b):
    return lax.dot_general(a, b, (((1,), (1,)), ((), ())), preferred_element_type=F32)


def _ada_kernel(c_ref, w_ref, b_ref, o_ref):
    batch = c_ref.shape[0]
    tn = w_ref.shape[1]
    for b in range(batch):
        cb = c_ref[b]
        cond = cb * jax.nn.sigmoid(cb)
        prod = w_ref[...] * jnp.tile(cond, (1, tn // LANES))
        o_ref[b:b + 1, :] = jnp.sum(prod, axis=0, keepdims=True) + b_ref[...]
    o_ref[batch:, :] = jnp.zeros((o_ref.shape[0] - batch, tn), F32)


def _ada(c, ada_w, ada_b, tn=1024):
    depth, d, n = ada_w.shape
    b = c.shape[0]
    assert b < SUBLANES_F32 and n % d == 0 and d % tn == 0
    per_chunk = d // tn
    c_lanes = jnp.broadcast_to(c[:, :, None], (b, d, LANES))
    return pl.pallas_call(
        _ada_kernel,
        grid=(depth, n // tn),
        in_specs=[
            pl.BlockSpec((b, d, LANES), lambda l, j: (0, 0, 0)),
            pl.BlockSpec((None, d, tn), lambda l, j: (l, 0, j)),
            pl.BlockSpec((None, 1, tn), lambda l, j: (l, 0, j)),
        ],
        out_specs=pl.BlockSpec((None, None, SUBLANES_F32, tn),
                               lambda l, j: (l, j // per_chunk, 0, j % per_chunk)),
        out_shape=jax.ShapeDtypeStruct((depth, n // d, SUBLANES_F32, d), F32),
        compiler_params=pltpu.CompilerParams(
            dimension_semantics=("arbitrary", "arbitrary"), vmem_limit_bytes=VMEM_LIMIT_BYTES),
        name="ada",
    )(c_lanes, ada_w, ada_b.reshape(depth, 1, n))


MOD_SH1, MOD_SC1, MOD_G1, MOD_SH2, MOD_SC2, MOD_G2 = range(6)
NORM_PRE_MIX, NORM_POST_MIX, NORM_PRE_FFN, NORM_POST_FFN = range(4)


def _mod_spec(layer, chunk, d):
    return pl.BlockSpec((None, None, SUBLANES_F32, d), lambda *_: (layer, chunk, 0, 0))


def _norm_spec(layer, d):
    return pl.BlockSpec((None, 4, d), lambda *_: (layer, 0, 0))


def _batch_row(ref, b):
    return ref[pl.ds(b, 1), :]


NORM_STRIP = 32
NORM_UNROLL = 8


def _strip_rows(s):
    return pl.ds(pl.multiple_of(s * NORM_STRIP, NORM_STRIP), NORM_STRIP)


def _row_rsqrt_ms(src_ref, rs_ref):
    n_rows, d = src_ref.shape

    def body(s, carry):
        rows = _strip_rows(s)
        x = src_ref[rows, :]
        ms = jnp.sum(x * x, axis=1, keepdims=True) * (1.0 / d)
        rs_ref[rows, :] = jnp.broadcast_to(lax.rsqrt(ms + EPS), (NORM_STRIP, LANES))
        return carry

    lax.fori_loop(0, n_rows // NORM_STRIP, body, 0, unroll=NORM_UNROLL)


def _scaled_rows(src_ref, rs_ref, rows):
    return src_ref[rows, :] * jnp.tile(rs_ref[rows, :], (1, src_ref.shape[1] // LANES))


def _cast_pad_kernel(w_ref, o_ref, *, axis):
    n = w_ref.shape[axis]
    if axis == 0:
        o_ref[:n, :] = w_ref[...].astype(o_ref.dtype)
        o_ref[n:, :] = jnp.zeros((o_ref.shape[0] - n, o_ref.shape[1]), o_ref.dtype)
    else:
        o_ref[:, :n] = w_ref[...].astype(o_ref.dtype)
        o_ref[:, n:] = jnp.zeros((o_ref.shape[0], o_ref.shape[1] - n), o_ref.dtype)


def _prep_w_up(w_up, dff_pad, tr=256):
    depth, d, two_dff = w_up.shape
    dff = two_dff // 2
    return pl.pallas_call(
        functools.partial(_cast_pad_kernel, axis=1),
        grid=(depth, 2, d // tr),
        in_specs=[pl.BlockSpec((None, tr, dff), lambda l, p, r: (l, r, p))],
        out_specs=pl.BlockSpec((None, None, tr, dff_pad), lambda l, p, r: (l, p, r, 0)),
        out_shape=jax.ShapeDtypeStruct((depth, 2, d, dff_pad), BF16),
        compiler_params=pltpu.CompilerParams(
            dimension_semantics=("arbitrary",) * 3, vmem_limit_bytes=VMEM_LIMIT_BYTES),
        name="prep_w_up",
    )(w_up)


def _prep_w_down(w_down, dff_pad, tc=256):
    depth, dff, d = w_down.shape
    return pl.pallas_call(
        functools.partial(_cast_pad_kernel, axis=0),
        grid=(depth, d // tc),
        in_specs=[pl.BlockSpec((None, dff, tc), lambda l, c: (l, 0, c))],
        out_specs=pl.BlockSpec((None, dff_pad, tc), lambda l, c: (l, 0, c)),
        out_shape=jax.ShapeDtypeStruct((depth, dff_pad, d), BF16),
        compiler_params=pltpu.CompilerParams(
            dimension_semantics=("arbitrary",) * 2, vmem_limit_bytes=VMEM_LIMIT_BYTES),
        name="prep_w_down",
    )(w_down)


def _norm_proj_kernel(x_ref, g_ref, sc_ref, sh_ref, w_ref, cs_ref, o_ref, *rest, per_seq, slab_from):
    if slab_from is None:
        h_ref, rs_ref = rest
    else:
        slab_ref, h_ref, rs_ref = rest
    j = pl.program_id(1)

    @pl.when(j == 0)
    def _():
        b = pl.program_id(0) // per_seq
        g_mod = g_ref[NORM_PRE_MIX:NORM_PRE_MIX + 1, :] * (1.0 + _batch_row(sc_ref, b))
        shift = _batch_row(sh_ref, b)
        _row_rsqrt_ms(x_ref, rs_ref)

        def body(s, carry):
            rows = _strip_rows(s)
            h_ref[rows, :] = (_scaled_rows(x_ref, rs_ref, rows) * g_mod + shift).astype(BF16)
            return carry

        lax.fori_loop(0, x_ref.shape[0] // NORM_STRIP, body, 0, unroll=NORM_UNROLL)

    res = _dot(h_ref[...], w_ref[...]) * cs_ref[...]
    o_ref[...] = res.astype(o_ref.dtype)
    if slab_from is not None:
        @pl.when(j >= slab_from)
        def _():
            for hh in range(slab_ref.shape[0]):
                slab_ref[hh] = res[:, hh * LANES:(hh + 1) * LANES]


def _norm_proj(x, norm_g, mod, layer, w, wl, col_scale, seq, tm=1024, tn=512, slab_from=None):
    m, d = x.shape
    n = w.shape[2]
    tm = min(tm, seq)
    nj = n // tn
    out_specs = pl.BlockSpec((tm, tn), lambda i, j: (i, j))
    out_shape = jax.ShapeDtypeStruct((m, n), BF16)
    if slab_from is not None:
        per_block = tn // LANES
        out_specs = [out_specs, pl.BlockSpec((per_block, tm, LANES),
                                             lambda i, j: (jnp.maximum(j - slab_from, 0), i, 0))]
        out_shape = [out_shape, jax.ShapeDtypeStruct(((nj - slab_from) * per_block, m, LANES), F32)]
    return pl.pallas_call(
        functools.partial(_norm_proj_kernel, per_seq=seq // tm, slab_from=slab_from),
        grid=(m // tm, nj),
        in_specs=[
            pl.BlockSpec((tm, d), lambda i, j: (i, 0)),
            _norm_spec(layer, d),
            _mod_spec(layer, MOD_SC1, d),
            _mod_spec(layer, MOD_SH1, d),
            pl.BlockSpec((None, d, tn), lambda i, j: (wl, 0, j)),
            pl.BlockSpec((1, tn), lambda i, j: (0, j)),
        ],
        out_specs=out_specs,
        out_shape=out_shape,
        scratch_shapes=[pltpu.VMEM((tm, d), BF16), pltpu.VMEM((tm, LANES), F32)],
        compiler_params=pltpu.CompilerParams(
            dimension_semantics=("parallel", "arbitrary"), vmem_limit_bytes=VMEM_LIMIT_BYTES),
        name="norm_proj",
    )(x, norm_g, mod, mod, w, col_scale.reshape(1, n))


def _out_proj_kernel(*refs, per_seq, widths):
    a_refs = refs[:len(widths)]
    w_ref, x_ref, g_ref, gate_ref, o_ref = refs[len(widths):]
    b = pl.program_id(0) // per_seq
    y, k0 = None, 0
    for a_ref, k in zip(a_refs, widths):
        part = _dot(a_ref[...], w_ref[k0:k0 + k, :])
        y = part if y is None else y + part
        k0 += k
    g = g_ref[NORM_POST_MIX:NORM_POST_MIX + 1, :]
    o_ref[...] = x_ref[...] + _batch_row(gate_ref, b) * (_rms(y) * g)


def _out_proj(mixed, w, wl, x, norm_g, mod, layer, seq, tm=512):
    m = x.shape[0]
    widths = tuple(a.shape[1] for a in mixed)
    k, d = w.shape[1], w.shape[2]
    assert sum(widths) == k
    tm = min(tm, seq)
    return pl.pallas_call(
        functools.partial(_out_proj_kernel, per_seq=seq // tm, widths=widths),
        grid=(m // tm,),
        in_specs=[pl.BlockSpec((tm, kk), lambda i: (i, 0)) for kk in widths] + [
            pl.BlockSpec((None, k, d), lambda i: (wl, 0, 0)),
            pl.BlockSpec((tm, d), lambda i: (i, 0)),
            _norm_spec(layer, d),
            _mod_spec(layer, MOD_G1, d),
        ],
        out_specs=pl.BlockSpec((tm, d), lambda i: (i, 0)),
        out_shape=jax.ShapeDtypeStruct((m, d), F32),
        compiler_params=pltpu.CompilerParams(
            dimension_semantics=("parallel",), vmem_limit_bytes=VMEM_LIMIT_BYTES),
        name="out_proj",
    )(*mixed, w, x, norm_g, mod)


def _ffn_kernel(x_ref, xp_ref, xn_ref, g_ref, sc_ref, sh_ref, wg_ref, wu_ref, cw_ref, cb_ref,
                wd_ref, gate_ref, o_ref, h_ref, rs_ref, *, tm, per_seq):
    i = pl.program_id(0)
    f = pl.program_id(1)
    b = i // per_seq
    halo = FFN_HALO
    d = x_ref.shape[1]

    @pl.when(f == 0)
    def _():
        g_mod = g_ref[NORM_PRE_FFN:NORM_PRE_FFN + 1, :] * (1.0 + _batch_row(sc_ref, b))
        shift = _batch_row(sh_ref, b)

        keep_prev = jnp.where(i % per_seq == 0, 0.0, 1.0)
        keep_next = jnp.where(i % per_seq == per_seq - 1, 0.0, 1.0)
        zeros = jnp.zeros((halo - SUBLANES_F32, d), F32)
        hp = (_rms(xp_ref[...]) * g_mod + shift) * keep_prev
        hn = (_rms(xn_ref[...]) * g_mod + shift) * keep_next
        h_ref[0:halo, :] = jnp.concatenate([zeros, hp], axis=0).astype(BF16)
        h_ref[halo + tm:, :] = jnp.concatenate([hn, zeros], axis=0).astype(BF16)

        _row_rsqrt_ms(x_ref, rs_ref)

        def body(s, carry):
            rows = _strip_rows(s)
            dst = pl.ds(pl.multiple_of(halo + s * NORM_STRIP, SUBLANES_BF16), NORM_STRIP)
            h_ref[dst, :] = (_scaled_rows(x_ref, rs_ref, rows) * g_mod + shift).astype(BF16)
            return carry

        lax.fori_loop(0, tm // NORM_STRIP, body, 0, unroll=NORM_UNROLL)
        o_ref[...] = jnp.zeros_like(o_ref)

    rows = tm + 2 * halo
    gate = _dot(h_ref[...], wg_ref[...])
    up = _dot(h_ref[halo:halo + tm, :], wu_ref[...])
    g_prev = pltpu.roll(gate, 1, 0)[halo:halo + tm]
    g_next = pltpu.roll(gate, rows - 1, 0)[halo:halo + tm]
    g_mid = gate[halo:halo + tm]
    cw = cw_ref[...]
    z = g_prev * cw[0:1] + g_mid * cw[1:2] + g_next * cw[2:3] + cb_ref[...]
    act = jax.nn.gelu(z, approximate=True) * up
    o_ref[...] += _dot(act.astype(BF16), wd_ref[...])

    @pl.when(f == pl.num_programs(1) - 1)
    def _():
        g_gate = g_ref[NORM_POST_FFN:NORM_POST_FFN + 1, :] * _batch_row(gate_ref, b)
        _row_rsqrt_ms(o_ref, rs_ref)

        def body(s, carry):
            rows = _strip_rows(s)
            o_ref[rows, :] = x_ref[rows, :] + _scaled_rows(o_ref, rs_ref, rows) * g_gate
            return carry

        lax.fori_loop(0, tm // NORM_STRIP, body, 0, unroll=NORM_UNROLL)


def _ffn(x, norm_g, mod, layer, w_gate_up, conv_w, conv_b, w_down, seq, tm=1024, tf=FFN_TF):
    m, d = x.shape
    dff = w_down.shape[1]
    tm = min(tm, seq)
    per_seq = seq // tm
    nblk8 = m // SUBLANES_F32
    r8 = tm // SUBLANES_F32
    return pl.pallas_call(
        functools.partial(_ffn_kernel, tm=tm, per_seq=per_seq),
        grid=(m // tm, dff // tf),
        in_specs=[
            pl.BlockSpec((tm, d), lambda i, f: (i, 0)),
            pl.BlockSpec((SUBLANES_F32, d), lambda i, f: (jnp.maximum(i * r8 - 1, 0), 0)),
            pl.BlockSpec((SUBLANES_F32, d), lambda i, f: (jnp.minimum((i + 1) * r8, nblk8 - 1), 0)),
            _norm_spec(layer, d),
            _mod_spec(layer, MOD_SC2, d),
            _mod_spec(layer, MOD_SH2, d),
            pl.BlockSpec((None, None, d, tf), lambda i, f: (layer, 0, 0, f)),
            pl.BlockSpec((None, None, d, tf), lambda i, f: (layer, 1, 0, f)),
            pl.BlockSpec((None, CONV_WIDTH, tf), lambda i, f: (layer, 0, f)),
            pl.BlockSpec((None, 1, tf), lambda i, f: (layer, 0, f)),
            pl.BlockSpec((None, tf, d), lambda i, f: (layer, f, 0)),
            _mod_spec(layer, MOD_G2, d),
        ],
        out_specs=pl.BlockSpec((tm, d), lambda i, f: (i, 0), pipeline_mode=pl.Buffered(1)),
        out_shape=jax.ShapeDtypeStruct((m, d), F32),
        scratch_shapes=[pltpu.VMEM((tm + 2 * FFN_HALO, d), BF16), pltpu.VMEM((tm, LANES), F32)],
        compiler_params=pltpu.CompilerParams(
            dimension_semantics=("parallel", "arbitrary"), vmem_limit_bytes=VMEM_LIMIT_BYTES),
        name="ffn",
    )(x, x, x, norm_g, mod, mod, w_gate_up, w_gate_up, conv_w, conv_b, w_down, mod)


def _attn_a_kernel(c_ref, sink_ref, qa_ref, ka_ref, va_ref, o_ref, *, seq, wl):
    tq = AB_BLOCK_Q
    t0 = pl.program_id(1) * tq
    width = min(tq + 2 * A_RADIUS, seq)
    start = pl.multiple_of(jnp.clip(t0 - A_RADIUS, 0, seq - width), SUBLANES_BF16)
    qpos = t0 + lax.broadcasted_iota(jnp.int32, (tq, 1), 0)
    kpos = start + lax.broadcasted_iota(jnp.int32, (1, width), 1)
    dist = jnp.abs(qpos - kpos)
    valid = dist <= A_RADIUS
    distf = dist.astype(F32)
    for kvh in range(A_KV_HEADS):
        cols = slice(kvh * HEAD_DIM, (kvh + 1) * HEAD_DIM)
        kw = ka_ref[pl.ds(start, width), cols]
        vw = va_ref[pl.ds(start, width), cols]
        for gq in range(A_GROUP):
            hq = kvh * A_GROUP + gq
            u = _dot_nt(qa_ref[:, hq * HEAD_DIM:(hq + 1) * HEAD_DIM], kw)
            u = jnp.where(valid, u - c_ref[hq] * distf, NEG_INF)
            sink = sink_ref[wl, hq] * LOG2E
            m = jnp.maximum(u.max(axis=-1, keepdims=True), sink)
            p = jnp.exp2(u - m)
            den = p.sum(axis=-1, keepdims=True) + jnp.exp2(sink - m)
            o = _dot(p.astype(BF16), vw) / den
            o_ref[:, hq * HEAD_DIM:(hq + 1) * HEAD_DIM] = o.astype(o_ref.dtype)


def _attn_a(proj, sink, wl, batch, seq):
    tq = min(AB_BLOCK_Q, seq)
    nq = seq // tq
    wq = A_Q_HEADS * HEAD_DIM
    wka = A_KV_HEADS * HEAD_DIM
    assert wq % wka == 0
    c = jnp.asarray(_alibi_slopes(N_MIX_HEADS)[:A_Q_HEADS] * np.float32(LOG2E))
    smem = pl.BlockSpec(memory_space=pltpu.SMEM)
    return pl.pallas_call(
        functools.partial(_attn_a_kernel, seq=seq, wl=wl),
        grid=(batch, nq),
        in_specs=[
            smem, smem,
            pl.BlockSpec((tq, wq), lambda b, n: (b * nq + n, 0)),
            pl.BlockSpec((seq, wka), lambda b, n: (b, wq // wka)),
            pl.BlockSpec((seq, wka), lambda b, n: (b, wq // wka + 1)),
        ],
        out_specs=pl.BlockSpec((tq, wq), lambda b, n: (b * nq + n, 0)),
        out_shape=jax.ShapeDtypeStruct((batch * seq, wq), BF16),
        compiler_params=pltpu.CompilerParams(
            dimension_semantics=("parallel", "arbitrary"), vmem_limit_bytes=VMEM_LIMIT_BYTES),
        name="attn_a",
    )(c, sink, proj, proj, proj)


B_TILE_Q = 128
B_TILES_PER_STEP = 8


def _attn_b_kernel(c_ref, q0_ref, q1_ref, q2_ref, k_ref, v_ref, o_ref, og_ref, lse_ref, *, seq):
    hb = pl.program_id(1)
    tq = B_TILE_Q
    q_refs = (q0_ref, q1_ref, q2_ref)

    def tile(g, dil, radius, r, blk):
        length = seq // dil
        width = min(tq + 2 * radius, length)
        i0 = blk * tq
        start = jnp.clip(i0 - radius, 0, length - width)
        if dil == 1:
            qrows = pl.ds(pl.multiple_of(i0, tq), tq)
            krows = pl.ds(pl.multiple_of(start, radius), width)
        else:
            qrows = pl.ds(r + dil * i0, tq, stride=dil)
            krows = pl.ds(r + dil * start, width, stride=dil)
        q = q_refs[g][qrows, :].astype(BF16)
        k = k_ref[krows, :].astype(BF16)
        v = v_ref[krows, :].astype(BF16)
        dist = jnp.abs((i0 + lax.broadcasted_iota(jnp.int32, (tq, 1), 0))
                       - (start + lax.broadcasted_iota(jnp.int32, (1, width), 1)))
        slope = c_ref[g * B_HEADS + hb] * float(dil)
        u = jnp.where(dist <= radius, _dot_nt(q, k) - slope * dist.astype(F32), NEG_INF)
        m = u.max(axis=-1, keepdims=True)
        p = jnp.exp2(u - m)
        den = p.sum(axis=-1, keepdims=True)
        og_ref[g, qrows, :] = _dot(p.astype(BF16), v) / den
        lse_ref[g, qrows, :] = jnp.broadcast_to(m + jnp.log2(den), (tq, LANES))

    for g, (window_len, dil) in enumerate(B_PAIRS):
        radius = window_len // (2 * dil)
        per_class = (seq // dil) // tq
        n_tiles = dil * per_class

        def body(s, carry, g=g, dil=dil, radius=radius, per_class=per_class):
            for k in range(B_TILES_PER_STEP):
                t = s * B_TILES_PER_STEP + k
                tile(g, dil, radius, t // per_class, t % per_class)
            return carry

        lax.fori_loop(0, n_tiles // B_TILES_PER_STEP, body, 0)

    def merge(s, carry):
        rows = pl.ds(pl.multiple_of(s * tq, tq), tq)
        lse = [lse_ref[g, rows, :] for g in range(len(B_PAIRS))]
        top = functools.reduce(jnp.maximum, lse)
        wts = [jnp.exp2(x - top) for x in lse]
        num = sum(w * og_ref[g, rows, :] for g, w in enumerate(wts))
        o_ref[rows, :] = (num / sum(wts)).astype(o_ref.dtype)
        return carry

    lax.fori_loop(0, seq // tq, merge, 0, unroll=2)


def _attn_b(qkv, batch, seq):
    n_groups = len(B_PAIRS)
    assert all(seq % (d * B_TILE_Q) == 0 and (seq // d // B_TILE_Q * d) % B_TILES_PER_STEP == 0
               for _, d in B_PAIRS)
    c = jnp.asarray(_alibi_slopes(N_MIX_HEADS)[A_Q_HEADS:] * np.float32(LOG2E))

    def slab(first):
        return pl.BlockSpec((None, seq, HEAD_DIM), lambda b, h: (first + h, b, 0))

    return pl.pallas_call(
        functools.partial(_attn_b_kernel, seq=seq),
        grid=(batch, B_HEADS),
        in_specs=[pl.BlockSpec(memory_space=pltpu.SMEM)]
        + [slab(g * B_HEADS) for g in range(n_groups)]
        + [slab(n_groups * B_HEADS), slab((n_groups + 1) * B_HEADS)],
        out_specs=pl.BlockSpec((seq, HEAD_DIM), lambda b, h: (b, h)),
        out_shape=jax.ShapeDtypeStruct((batch * seq, B_HEADS * HEAD_DIM), BF16),
        scratch_shapes=[pltpu.VMEM((n_groups, seq, HEAD_DIM), F32),
                        pltpu.VMEM((n_groups, seq, LANES), F32)],
        compiler_params=pltpu.CompilerParams(
            dimension_semantics=("parallel", "arbitrary"), vmem_limit_bytes=VMEM_LIMIT_BYTES),
        name="attn_b",
    )(c, qkv, qkv, qkv, qkv, qkv)


C_POS_BITS = 6
C_SLOPE_PARTS = 3
C_AUG_POS = 2 * C_SLOPE_PARTS
C_STRIP = 32


def _attn_c_consts():
    c = (_alibi_slopes(C_HEADS).astype(np.float64) * math.log2(math.e)).astype(np.float32)
    parts, rest = [], c.copy()
    for _ in range(C_SLOPE_PARTS):
        p = rest.astype(BF16).astype(np.float32)
        parts.append(p)
        rest = rest - p
    qc = np.zeros((C_HEADS, 1, LANES), np.float32)
    kc = np.zeros((C_HEADS, 1, LANES), np.float32)
    for a, p in enumerate(parts):
        qc[:, 0, C_AUG_POS + 2 * a] = p
        qc[:, 0, C_AUG_POS + 2 * a + 1] = p
        kc[:, 0, 2 * a] = p * (1 << C_POS_BITS)
        kc[:, 0, 2 * a + 1] = p
    return jnp.asarray(c), jnp.asarray(qc), jnp.asarray(kc)


def _attn_c_kernel(c_ref, qc_ref, kc_ref, q_ref, k_ref, v_ref, lam_ref, sg_ref, o_ref,
                   kaug_ref, qv_ref, bias_ref, u_ref, p_ref, m_ref, l_ref, alpha_ref, acc_ref,
                   *, tq, tk, seq, lambda_init):
    h = pl.program_id(1)
    qi = pl.program_id(2)
    nk = seq // tk
    ratio = tk // tq
    kd = qi // ratio
    off = qi - kd * ratio
    lane = lax.broadcasted_iota(jnp.int32, (1, LANES), 1)
    hi_lane = (lane & 1) == 0
    lo_mask = (1 << C_POS_BITS) - 1

    @pl.when(qi == 0)
    def _():
        key_pos_lane = (lane >= C_AUG_POS) & (lane < 2 * C_AUG_POS)

        def fill(kb, carry):
            rows = pl.ds(pl.multiple_of(kb * tk, tk), tk)
            pos = kb * tk + lax.broadcasted_iota(jnp.int32, (tk, 1), 0)
            hi = (pos - (pos & lo_mask)).astype(F32)
            lo = (pos & lo_mask).astype(F32)
            aug = jnp.where(key_pos_lane, -jnp.where(hi_lane, hi, lo), kc_ref[0]).astype(BF16)
            for j in range(2):
                kaug_ref[j, rows, 0:HEAD_DIM] = k_ref[rows, j * HEAD_DIM:(j + 1) * HEAD_DIM]
                kaug_ref[j, rows, HEAD_DIM:2 * HEAD_DIM] = aug
            return carry

        lax.fori_loop(0, nk, fill, 0)
        jj = lax.broadcasted_iota(jnp.int32, (1, tk), 1)
        for o in range(ratio):
            ii = o * tq + lax.broadcasted_iota(jnp.int32, (tq, 1), 0)
            bias_ref[o] = c_ref[h] * jnp.abs(ii - jj).astype(F32)

    qpos = qi * tq + lax.broadcasted_iota(jnp.int32, (tq, 1), 0)
    q_hi = (qpos >> C_POS_BITS).astype(F32)
    q_lo = (qpos & lo_mask).astype(F32)
    q_aug = jnp.where(lane < C_AUG_POS, jnp.where(hi_lane, q_hi, q_lo), qc_ref[0])
    for j in range(2):
        qj = q_ref[:, j * HEAD_DIM:(j + 1) * HEAD_DIM]
        for variant, aug in enumerate((-q_aug, q_aug, jnp.zeros_like(q_aug))):
            qv_ref[variant, j, :, 0:HEAD_DIM] = qj
            qv_ref[variant, j, :, HEAD_DIM:2 * HEAD_DIM] = aug.astype(BF16)

    def key_rows(step):
        kb = kd + step
        kb = jnp.where(kb >= nk, kb - nk, kb)
        return kb, pl.ds(pl.multiple_of(kb * tk, tk), tk)

    def scores(step, slot):
        kb, rows = key_rows(step)
        variant = 2 if step == 0 else jnp.where(kb < kd, 0, 1)
        for j in range(2):
            u_ref[slot, j] = _dot_nt(qv_ref[variant, j], kaug_ref[j, rows, :])

    def accumulate(step, slot):
        _, rows = key_rows(step)
        vblk = v_ref[rows, :]
        for j in range(2):
            for r0 in range(0, tq, C_STRIP):
                rs = slice(r0, r0 + C_STRIP)
                u = u_ref[slot, j, rs, :]
                if step == 0:
                    u = u - bias_ref[off, rs, :]
                    m_new = jnp.broadcast_to(jnp.max(u, axis=1, keepdims=True), (C_STRIP, LANES))
                else:
                    m_prev = m_ref[j, rs, :]
                    m_new = jnp.maximum(m_prev, jnp.max(u, axis=1, keepdims=True))
                    alpha = jnp.exp2(m_prev - m_new)
                    alpha_ref[j, rs, :] = alpha
                p = jnp.exp2(u - jnp.tile(m_new, (1, tk // LANES)))
                psum = p[:, 0:LANES]
                for cb in range(1, tk // LANES):
                    psum = psum + p[:, cb * LANES:(cb + 1) * LANES]
                l_ref[j, rs, :] = psum if step == 0 else alpha * l_ref[j, rs, :] + psum
                m_ref[j, rs, :] = m_new
                p_ref[slot, j, rs, :] = p.astype(BF16)
            pv = _dot(p_ref[slot, j], vblk)
            acc_ref[j] = pv if step == 0 else jnp.tile(alpha_ref[j], (1, 2)) * acc_ref[j] + pv

    scores(0, 0)
    for step in range(nk):
        if step + 1 < nk:
            scores(step + 1, (step + 1) % 2)
        accumulate(step, step % 2)

    lp = lam_ref[...]
    lam = (jnp.exp(jnp.sum(lp[0:1] * lp[1:2], axis=-1, keepdims=True))
           - jnp.exp(jnp.sum(lp[2:3] * lp[3:4], axis=-1, keepdims=True)) + lambda_init)
    den0 = jnp.sum(l_ref[0], axis=1, keepdims=True)
    den1 = jnp.sum(l_ref[1], axis=1, keepdims=True)
    o = acc_ref[0] / den0 - lam * (acc_ref[1] / den1)
    o = (_rms(o) * sg_ref[...]) * (1.0 - lambda_init)
    o_ref[...] = o.astype(o_ref.dtype)


def _attn_c(proj, lam_params, subln_g, wl, lambda_init, batch, seq, tq=512, tk=1024):
    wh = 2 * HEAD_DIM
    d = C_HEADS * wh
    tk = min(tk, seq)
    tq = min(tq, tk)
    nq = seq // tq
    assert seq <= (1 << (2 * C_POS_BITS)) and tk % tq == 0 and tq % C_STRIP == 0 and tk % LANES == 0
    c, qc, kc = _attn_c_consts()
    const_spec = pl.BlockSpec((1, 1, LANES), lambda b, h, qi: (h, 0, 0))
    return pl.pallas_call(
        functools.partial(_attn_c_kernel, tq=tq, tk=tk, seq=seq, lambda_init=lambda_init),
        grid=(batch, C_HEADS, nq),
        in_specs=[
            pl.BlockSpec(memory_space=pltpu.SMEM),
            const_spec, const_spec,
            pl.BlockSpec((tq, wh), lambda b, h, qi: (b * nq + qi, h)),
            pl.BlockSpec((seq, wh), lambda b, h, qi: (b, C_HEADS + h)),
            pl.BlockSpec((seq, wh), lambda b, h, qi: (b, 2 * C_HEADS + h)),
            pl.BlockSpec((None, 4, HEAD_DIM), lambda b, h, qi: (wl, 0, 0)),
            pl.BlockSpec((None, 1, wh), lambda b, h, qi: (wl, 0, 0)),
        ],
        out_specs=pl.BlockSpec((tq, wh), lambda b, h, qi: (b * nq + qi, h)),
        out_shape=jax.ShapeDtypeStruct((batch * seq, d), BF16),
        scratch_shapes=[
            pltpu.VMEM((2, seq, wh), BF16),
            pltpu.VMEM((3, 2, tq, wh), BF16),
            pltpu.VMEM((tk // tq, tq, tk), F32),
            pltpu.VMEM((2, 2, tq, tk), F32),
            pltpu.VMEM((2, 2, tq, tk), BF16),
            pltpu.VMEM((2, tq, LANES), F32),
            pltpu.VMEM((2, tq, LANES), F32),
            pltpu.VMEM((2, tq, LANES), F32),
            pltpu.VMEM((2, tq, wh), F32),
        ],
        compiler_params=pltpu.CompilerParams(
            dimension_semantics=("parallel", "parallel", "arbitrary"),
            vmem_limit_bytes=VMEM_LIMIT_BYTES),
        name="attn_c",
    )(c, qc, kc, proj, proj, proj, lam_params, subln_g.reshape(-1, 1, wh))


def kernel(x, c, ada_w, ada_b, norm_g, ab_w_in, ab_w_out, a_sink, c_w_in, c_w_out, c_lambda,
           c_subln_g, ffn_w_up, ffn_conv_w, ffn_conv_b, ffn_w_down):
    batch, seq, d = x.shape
    depth = ada_w.shape[0]
    dff = ffn_w_down.shape[1]
    dff_pad = -(-dff // FFN_TF) * FFN_TF

    ab_w_in_b, ab_w_out_b = ab_w_in.astype(BF16), ab_w_out.astype(BF16)
    c_w_in_b, c_w_out_b = c_w_in.astype(BF16), c_w_out.astype(BF16)
    w_gate_up = _prep_w_up(ffn_w_up, dff_pad)
    w_down = _prep_w_down(ffn_w_down, dff_pad)
    conv_w = jnp.pad(ffn_conv_w, ((0, 0), (0, 0), (0, dff_pad - dff)))
    conv_b = jnp.pad(ffn_conv_b, ((0, 0), (0, dff_pad - dff))).reshape(depth, 1, dff_pad)

    q_scale = np.float32(HEAD_DIM ** -0.5 * LOG2E)
    ab_scale = np.ones((ab_w_in.shape[2],), np.float32)
    wq, wkv_a = A_Q_HEADS * HEAD_DIM, 2 * A_KV_HEADS * HEAD_DIM
    ab_scale[:wq] = q_scale
    ab_scale[wq + wkv_a:wq + wkv_a + len(B_PAIRS) * B_HEADS * HEAD_DIM] = q_scale
    ab_col_scale = jnp.asarray(ab_scale)
    c_col_scale = jnp.asarray(np.concatenate([np.full((d,), q_scale, np.float32),
                                              np.ones((2 * d,), np.float32)]))

    mod = _ada(c, ada_w, ada_b)
    xs = x.reshape(batch * seq, d)
    for layer in range(depth):
        wl = layer // 2
        if layer % 2 == 0:
            proj, slabs = _norm_proj(xs, norm_g, mod, layer, ab_w_in_b, wl, ab_col_scale, seq,
                                     tn=AB_PROJ_TN, slab_from=AB_SLAB_FROM)
            mixed = (_attn_a(proj, a_sink, wl, batch, seq), _attn_b(slabs, batch, seq))
            xs = _out_proj(mixed, ab_w_out_b, wl, xs, norm_g, mod, layer, seq)
        else:
            lambda_init = 0.8 - 0.6 * math.exp(-0.3 * layer)
            proj = _norm_proj(xs, norm_g, mod, layer, c_w_in_b, wl, c_col_scale, seq, tn=C_PROJ_TN)
            mixed = _attn_c(proj, c_lambda, c_subln_g, wl, lambda_init, batch, seq)
            xs = _out_proj((mixed,), c_w_out_b, wl, xs, norm_g, mod, layer, seq)
        xs = _ffn(xs, norm_g, mod, layer, w_gate_up, conv_w, conv_b, w_down, seq)
    return xs.reshape(batch, seq, d)
```

```python
import functools
import math

import numpy as np
import jax
import jax.numpy as jnp
from jax import lax
from jax.experimental import pallas as pl
from jax.experimental.pallas import tpu as pltpu

F32 = jnp.float32
BF16 = jnp.bfloat16

HEAD_DIM = 128
A_Q_HEADS = 4
A_KV_HEADS = 2
A_GROUP = A_Q_HEADS // A_KV_HEADS
A_RADIUS = 128
B_PAIRS = ((128, 1), (512, 4), (2048, 16))
B_HEADS = 4
N_MIX_HEADS = A_Q_HEADS + len(B_PAIRS) * B_HEADS
C_HEADS = 8
CONV_WIDTH = 3
EPS = 1e-6
NEG_INF = -1e30

SUBLANES_F32 = 8
SUBLANES_BF16 = 16
LANES = 128
VMEM_LIMIT_BYTES = 56 * 1024 * 1024

AB_BLOCK_Q = 256
FFN_HALO = SUBLANES_BF16
FFN_TF = 512
AB_PROJ_TM = 512
AB_PROJ_TN = 1792
C_PROJ_TN = 1536
LOG2E = math.log2(math.e)


def _alibi_slopes(n):
    return (2.0 ** (-8.0 * np.arange(1, n + 1, dtype=np.float32) / n)).astype(np.float32)


def _rms(x):
    return x * lax.rsqrt(jnp.mean(x * x, axis=-1, keepdims=True) + EPS)


def _dot(a, b):
    return jnp.dot(a, b, preferred_element_type=F32)


def _dot_nt(a, b):
    return lax.dot_general(a, b, (((1,), (1,)), ((), ())), preferred_element_type=F32)


def _ada_kernel(c_ref, w_ref, b_ref, o_ref):
    batch = c_ref.shape[0]
    tn = w_ref.shape[1]
    for b in range(batch):
        cb = c_ref[b]
        cond = cb * jax.nn.sigmoid(cb)
        prod = w_ref[...] * jnp.tile(cond, (1, tn // LANES))
        o_ref[b:b + 1, :] = jnp.sum(prod, axis=0, keepdims=True) + b_ref[...]
    o_ref[batch:, :] = jnp.zeros((o_ref.shape[0] - batch, tn), F32)


def _ada(c, ada_w, ada_b, tn=1024):
    depth, d, n = ada_w.shape
    b = c.shape[0]
    assert b < SUBLANES_F32 and n % d == 0 and d % tn == 0
    per_chunk = d // tn
    c_lanes = jnp.broadcast_to(c[:, :, None], (b, d, LANES))
    return pl.pallas_call(
        _ada_kernel,
        grid=(depth, n // tn),
        in_specs=[
            pl.BlockSpec((b, d, LANES), lambda l, j: (0, 0, 0)),
            pl.BlockSpec((None, d, tn), lambda l, j: (l, 0, j)),
            pl.BlockSpec((None, 1, tn), lambda l, j: (l, 0, j)),
        ],
        out_specs=pl.BlockSpec((None, None, SUBLANES_F32, tn),
                               lambda l, j: (l, j // per_chunk, 0, j % per_chunk)),
        out_shape=jax.ShapeDtypeStruct((depth, n // d, SUBLANES_F32, d), F32),
        compiler_params=pltpu.CompilerParams(
            dimension_semantics=("arbitrary", "arbitrary"), vmem_limit_bytes=VMEM_LIMIT_BYTES),
        name="ada",
    )(c_lanes, ada_w, ada_b.reshape(depth, 1, n))


MOD_SH1, MOD_SC1, MOD_G1, MOD_SH2, MOD_SC2, MOD_G2 = range(6)
NORM_PRE_MIX, NORM_POST_MIX, NORM_PRE_FFN, NORM_POST_FFN = range(4)


def _mod_spec(layer, chunk, d):
    return pl.BlockSpec((None, None, SUBLANES_F32, d), lambda *_: (layer, chunk, 0, 0))


def _norm_spec(layer, d):
    return pl.BlockSpec((None, 4, d), lambda *_: (layer, 0, 0))


def _batch_row(ref, b):
    return ref[pl.ds(b, 1), :]


NORM_STRIP = 32
NORM_UNROLL = 8


def _strip_rows(s):
    return pl.ds(pl.multiple_of(s * NORM_STRIP, NORM_STRIP), NORM_STRIP)


def _row_rsqrt_ms(src_ref, rs_ref):
    n_rows, d = src_ref.shape

    def body(s, carry):
        rows = _strip_rows(s)
        x = src_ref[rows, :]
        ms = jnp.sum(x * x, axis=1, keepdims=True) * (1.0 / d)
        rs_ref[rows, :] = jnp.broadcast_to(lax.rsqrt(ms + EPS), (NORM_STRIP, LANES))
        return carry

    lax.fori_loop(0, n_rows // NORM_STRIP, body, 0, unroll=NORM_UNROLL)


def _scaled_rows(src_ref, rs_ref, rows):
    return src_ref[rows, :] * jnp.tile(rs_ref[rows, :], (1, src_ref.shape[1] // LANES))


def _cast_pad_kernel(w_ref, o_ref, *, axis):
    n = w_ref.shape[axis]
    if axis == 0:
        o_ref[:n, :] = w_ref[...].astype(o_ref.dtype)
        o_ref[n:, :] = jnp.zeros((o_ref.shape[0] - n, o_ref.shape[1]), o_ref.dtype)
    else:
        o_ref[:, :n] = w_ref[...].astype(o_ref.dtype)
        o_ref[:, n:] = jnp.zeros((o_ref.shape[0], o_ref.shape[1] - n), o_ref.dtype)


def _prep_w_up(w_up, dff_pad, tr=256):
    depth, d, two_dff = w_up.shape
    dff = two_dff // 2
    return pl.pallas_call(
        functools.partial(_cast_pad_kernel, axis=1),
        grid=(depth, 2, d // tr),
        in_specs=[pl.BlockSpec((None, tr, dff), lambda l, p, r: (l, r, p))],
        out_specs=pl.BlockSpec((None, None, tr, dff_pad), lambda l, p, r: (l, p, r, 0)),
        out_shape=jax.ShapeDtypeStruct((depth, 2, d, dff_pad), BF16),
        compiler_params=pltpu.CompilerParams(
            dimension_semantics=("arbitrary",) * 3, vmem_limit_bytes=VMEM_LIMIT_BYTES),
        name="prep_w_up",
    )(w_up)


def _prep_w_down(w_down, dff_pad, tc=256):
    depth, dff, d = w_down.shape
    return pl.pallas_call(
        functools.partial(_cast_pad_kernel, axis=0),
        grid=(depth, d // tc),
        in_specs=[pl.BlockSpec((None, dff, tc), lambda l, c: (l, 0, c))],
        out_specs=pl.BlockSpec((None, dff_pad, tc), lambda l, c: (l, 0, c)),
        out_shape=jax.ShapeDtypeStruct((depth, dff_pad, d), BF16),
        compiler_params=pltpu.CompilerParams(
            dimension_semantics=("arbitrary",) * 2, vmem_limit_bytes=VMEM_LIMIT_BYTES),
        name="prep_w_down",
    )(w_down)


def _norm_proj_kernel(x_ref, g_ref, sc_ref, sh_ref, w_ref, cs_ref, o_ref, *rest, per_seq, slab_from):
    if slab_from is None:
        h_ref, rs_ref = rest
    else:
        slab_ref, h_ref, rs_ref = rest
    j = pl.program_id(1)

    @pl.when(j == 0)
    def _():
        b = pl.program_id(0) // per_seq
        g_mod = g_ref[NORM_PRE_MIX:NORM_PRE_MIX + 1, :] * (1.0 + _batch_row(sc_ref, b))
        shift = _batch_row(sh_ref, b)
        _row_rsqrt_ms(x_ref, rs_ref)

        def body(s, carry):
            rows = _strip_rows(s)
            h_ref[rows, :] = (_scaled_rows(x_ref, rs_ref, rows) * g_mod + shift).astype(BF16)
            return carry

        lax.fori_loop(0, x_ref.shape[0] // NORM_STRIP, body, 0, unroll=NORM_UNROLL)

    res = _dot(h_ref[...], w_ref[...]) * cs_ref[...]
    o_ref[...] = res.astype(o_ref.dtype)
    if slab_from is not None:
        @pl.when(j >= slab_from)
        def _():
            for hh in range(slab_ref.shape[0]):
                slab_ref[hh] = res[:, hh * LANES:(hh + 1) * LANES]


def _norm_proj(x, norm_g, mod, layer, w, wl, col_scale, seq, tm=1024, tn=512, slab_from=None):
    m, d = x.shape
    n = w.shape[2]
    tm = min(tm, seq)
    nj = n // tn
    out_specs = pl.BlockSpec((tm, tn), lambda i, j: (i, j))
    out_shape = jax.ShapeDtypeStruct((m, n), BF16)
    if slab_from is not None:
        per_block = tn // LANES
        out_specs = [out_specs, pl.BlockSpec((per_block, tm, LANES),
                                             lambda i, j: (jnp.maximum(j - slab_from, 0), i, 0))]
        out_shape = [out_shape, jax.ShapeDtypeStruct(((nj - slab_from) * per_block, m, LANES), F32)]
    return pl.pallas_call(
        functools.partial(_norm_proj_kernel, per_seq=seq // tm, slab_from=slab_from),
        grid=(m // tm, nj),
        in_specs=[
            pl.BlockSpec((tm, d), lambda i, j: (i, 0)),
            _norm_spec(layer, d),
            _mod_spec(layer, MOD_SC1, d),
            _mod_spec(layer, MOD_SH1, d),
            pl.BlockSpec((None, d, tn), lambda i, j: (wl, 0, j)),
            pl.BlockSpec((1, tn), lambda i, j: (0, j)),
        ],
        out_specs=out_specs,
        out_shape=out_shape,
        scratch_shapes=[pltpu.VMEM((tm, d), BF16), pltpu.VMEM((tm, LANES), F32)],
        compiler_params=pltpu.CompilerParams(
            dimension_semantics=("parallel", "arbitrary"), vmem_limit_bytes=VMEM_LIMIT_BYTES),
        name="norm_proj",
    )(x, norm_g, mod, mod, w, col_scale.reshape(1, n))


def _out_proj_kernel(*refs, per_seq, widths):
    a_refs = refs[:len(widths)]
    w_ref, x_ref, g_ref, gate_ref, o_ref = refs[len(widths):]
    b = pl.program_id(0) // per_seq
    y, k0 = None, 0
    for a_ref, k in zip(a_refs, widths):
        part = _dot(a_ref[...], w_ref[k0:k0 + k, :])
        y = part if y is None else y + part
        k0 += k
    g = g_ref[NORM_POST_MIX:NORM_POST_MIX + 1, :]
    o_ref[...] = x_ref[...] + _batch_row(gate_ref, b) * (_rms(y) * g)


def _out_proj(mixed, w, wl, x, norm_g, mod, layer, seq, tm=512):
    m = x.shape[0]
    widths = tuple(a.shape[1] for a in mixed)
    k, d = w.shape[1], w.shape[2]
    assert sum(widths) == k
    tm = min(tm, seq)
    return pl.pallas_call(
        functools.partial(_out_proj_kernel, per_seq=seq // tm, widths=widths),
        grid=(m // tm,),
        in_specs=[pl.BlockSpec((tm, kk), lambda i: (i, 0)) for kk in widths] + [
            pl.BlockSpec((None, k, d), lambda i: (wl, 0, 0)),
            pl.BlockSpec((tm, d), lambda i: (i, 0)),
            _norm_spec(layer, d),
            _mod_spec(layer, MOD_G1, d),
        ],
        out_specs=pl.BlockSpec((tm, d), lambda i: (i, 0)),
        out_shape=jax.ShapeDtypeStruct((m, d), F32),
        compiler_params=pltpu.CompilerParams(
            dimension_semantics=("parallel",), vmem_limit_bytes=VMEM_LIMIT_BYTES),
        name="out_proj",
    )(*mixed, w, x, norm_g, mod)


def _ffn_kernel(x_ref, xp_ref, xn_ref, g_ref, sc_ref, sh_ref, wg_ref, wu_ref, cw_ref, cb_ref,
                wd_ref, gate_ref, o_ref, h_ref, rs_ref, *, tm, per_seq):
    i = pl.program_id(0)
    f = pl.program_id(1)
    b = i // per_seq
    halo = FFN_HALO
    d = x_ref.shape[1]

    @pl.when(f == 0)
    def _():
        g_mod = g_ref[NORM_PRE_FFN:NORM_PRE_FFN + 1, :] * (1.0 + _batch_row(sc_ref, b))
        shift = _batch_row(sh_ref, b)

        keep_prev = jnp.where(i % per_seq == 0, 0.0, 1.0)
        keep_next = jnp.where(i % per_seq == per_seq - 1, 0.0, 1.0)
        zeros = jnp.zeros((halo - SUBLANES_F32, d), F32)
        hp = (_rms(xp_ref[...]) * g_mod + shift) * keep_prev
        hn = (_rms(xn_ref[...]) * g_mod + shift) * keep_next
        h_ref[0:halo, :] = jnp.concatenate([zeros, hp], axis=0).astype(BF16)
        h_ref[halo + tm:, :] = jnp.concatenate([hn, zeros], axis=0).astype(BF16)

        _row_rsqrt_ms(x_ref, rs_ref)

        def body(s, carry):
            rows = _strip_rows(s)
            dst = pl.ds(pl.multiple_of(halo + s * NORM_STRIP, SUBLANES_BF16), NORM_STRIP)
            h_ref[dst, :] = (_scaled_rows(x_ref, rs_ref, rows) * g_mod + shift).astype(BF16)
            return carry

        lax.fori_loop(0, tm // NORM_STRIP, body, 0, unroll=NORM_UNROLL)
        o_ref[...] = jnp.zeros_like(o_ref)

    rows = tm + 2 * halo
    gate = _dot(h_ref[...], wg_ref[...])
    up = _dot(h_ref[halo:halo + tm, :], wu_ref[...])
    g_prev = pltpu.roll(gate, 1, 0)[halo:halo + tm]
    g_next = pltpu.roll(gate, rows - 1, 0)[halo:halo + tm]
    g_mid = gate[halo:halo + tm]
    cw = cw_ref[...]
    z = g_prev * cw[0:1] + g_mid * cw[1:2] + g_next * cw[2:3] + cb_ref[...]
    act = jax.nn.gelu(z, approximate=True) * up
    o_ref[...] += _dot(act.astype(BF16), wd_ref[...])

    @pl.when(f == pl.num_programs(1) - 1)
    def _():
        g_gate = g_ref[NORM_POST_FFN:NORM_POST_FFN + 1, :] * _batch_row(gate_ref, b)
        _row_rsqrt_ms(o_ref, rs_ref)

        def body(s, carry):
            rows = _strip_rows(s)
            o_ref[rows, :] = x_ref[rows, :] + _scaled_rows(o_ref, rs_ref, rows) * g_gate
            return carry

        lax.fori_loop(0, tm // NORM_STRIP, body, 0, unroll=NORM_UNROLL)


def _ffn(x, norm_g, mod, layer, w_gate_up, conv_w, conv_b, w_down, seq, tm=1024, tf=FFN_TF):
    m, d = x.shape
    dff = w_down.shape[1]
    tm = min(tm, seq)
    per_seq = seq // tm
    nblk8 = m // SUBLANES_F32
    r8 = tm // SUBLANES_F32
    return pl.pallas_call(
        functools.partial(_ffn_kernel, tm=tm, per_seq=per_seq),
        grid=(m // tm, dff // tf),
        in_specs=[
            pl.BlockSpec((tm, d), lambda i, f: (i, 0)),
            pl.BlockSpec((SUBLANES_F32, d), lambda i, f: (jnp.maximum(i * r8 - 1, 0), 0)),
            pl.BlockSpec((SUBLANES_F32, d), lambda i, f: (jnp.minimum((i + 1) * r8, nblk8 - 1), 0)),
            _norm_spec(layer, d),
            _mod_spec(layer, MOD_SC2, d),
            _mod_spec(layer, MOD_SH2, d),
            pl.BlockSpec((None, None, d, tf), lambda i, f: (layer, 0, 0, f)),
            pl.BlockSpec((None, None, d, tf), lambda i, f: (layer, 1, 0, f)),
            pl.BlockSpec((None, CONV_WIDTH, tf), lambda i, f: (layer, 0, f)),
            pl.BlockSpec((None, 1, tf), lambda i, f: (layer, 0, f)),
            pl.BlockSpec((None, tf, d), lambda i, f: (layer, f, 0)),
            _mod_spec(layer, MOD_G2, d),
        ],
        out_specs=pl.BlockSpec((tm, d), lambda i, f: (i, 0), pipeline_mode=pl.Buffered(1)),
        out_shape=jax.ShapeDtypeStruct((m, d), F32),
        scratch_shapes=[pltpu.VMEM((tm + 2 * FFN_HALO, d), BF16), pltpu.VMEM((tm, LANES), F32)],
        compiler_params=pltpu.CompilerParams(
            dimension_semantics=("parallel", "arbitrary"), vmem_limit_bytes=VMEM_LIMIT_BYTES),
        name="ffn",
    )(x, x, x, norm_g, mod, mod, w_gate_up, w_gate_up, conv_w, conv_b, w_down, mod)


def _attn_a_kernel(c_ref, sink_ref, qa_ref, ka_ref, va_ref, o_ref, *, seq, wl):
    tq = AB_BLOCK_Q
    t0 = pl.program_id(1) * tq
    width = min(tq + 2 * A_RADIUS, seq)
    start = pl.multiple_of(jnp.clip(t0 - A_RADIUS, 0, seq - width), SUBLANES_BF16)
    qpos = t0 + lax.broadcasted_iota(jnp.int32, (tq, 1), 0)
    kpos = start + lax.broadcasted_iota(jnp.int32, (1, width), 1)
    dist = jnp.abs(qpos - kpos)
    valid = dist <= A_RADIUS
    distf = dist.astype(F32)
    for kvh in range(A_KV_HEADS):
        cols = slice(kvh * HEAD_DIM, (kvh + 1) * HEAD_DIM)
        kw = ka_ref[pl.ds(start, width), cols]
        vw = va_ref[pl.ds(start, width), cols]
        for gq in range(A_GROUP):
            hq = kvh * A_GROUP + gq
            u = _dot_nt(qa_ref[:, hq * HEAD_DIM:(hq + 1) * HEAD_DIM], kw)
            u = jnp.where(valid, u - c_ref[hq] * distf, NEG_INF)
            sink = sink_ref[wl, hq] * LOG2E
            m = jnp.maximum(u.max(axis=-1, keepdims=True), sink)
            p = jnp.exp2(u - m)
            den = p.sum(axis=-1, keepdims=True) + jnp.exp2(sink - m)
            o = _dot(p.astype(BF16), vw) / den
            o_ref[:, hq * HEAD_DIM:(hq + 1) * HEAD_DIM] = o.astype(o_ref.dtype)


def _attn_a(proj, sink, wl, batch, seq):
    tq = min(AB_BLOCK_Q, seq)
    nq = seq // tq
    wq = A_Q_HEADS * HEAD_DIM
    wka = A_KV_HEADS * HEAD_DIM
    assert wq % wka == 0
    c = jnp.asarray(_alibi_slopes(N_MIX_HEADS)[:A_Q_HEADS] * np.float32(LOG2E))
    smem = pl.BlockSpec(memory_space=pltpu.SMEM)
    return pl.pallas_call(
        functools.partial(_attn_a_kernel, seq=seq, wl=wl),
        grid=(batch, nq),
        in_specs=[
            smem, smem,
            pl.BlockSpec((tq, wq), lambda b, n: (b * nq + n, 0)),
            pl.BlockSpec((seq, wka), lambda b, n: (b, wq // wka)),
            pl.BlockSpec((seq, wka), lambda b, n: (b, wq // wka + 1)),
        ],
        out_specs=pl.BlockSpec((tq, wq), lambda b, n: (b * nq + n, 0)),
        out_shape=jax.ShapeDtypeStruct((batch * seq, wq), BF16),
        compiler_params=pltpu.CompilerParams(
            dimension_semantics=("parallel", "arbitrary"), vmem_limit_bytes=VMEM_LIMIT_BYTES),
        name="attn_a",
    )(c, sink, proj, proj, proj)


B_TILE_Q = 128
B_FOLD = 4
B_TILES_PER_STEP = 16


def _attn_b_kernel(c_ref, q0_ref, q1_ref, q2_ref, k_ref, v_ref, o_ref, og_ref, lse_ref, bias_ref,
                   fold_ref, ofold_ref, *, seq):
    hb = pl.program_id(1)
    tq = B_TILE_Q
    q_refs = (q0_ref, q1_ref, q2_ref)
    seg = seq // B_FOLD

    for g, (window_len, dil) in enumerate(B_PAIRS):
        radius = window_len // (2 * dil)
        width = min(tq + 2 * radius, seq // dil)
        slope = c_ref[g * B_HEADS + hb] * float(dil)
        for variant, offset in enumerate((0, radius, width - tq)):
            dist = jnp.abs((offset + lax.broadcasted_iota(jnp.int32, (tq, 1), 0))
                           - lax.broadcasted_iota(jnp.int32, (1, width), 1))
            bias_ref[g, variant, :, 0:width] = jnp.where(dist <= radius,
                                                         -slope * dist.astype(F32), NEG_INF)

    def fold_rows(r, first, n):
        return pl.ds((r % B_FOLD) * seg + r // B_FOLD + B_FOLD * first, n, stride=B_FOLD)

    def tile(g, dil, radius, r, blk):
        length = seq // dil
        width = min(tq + 2 * radius, length)
        i0 = blk * tq
        start = jnp.clip(i0 - radius, 0, length - width)
        if dil == 1:
            qrows = pl.ds(pl.multiple_of(i0, tq), tq)
            krows = pl.ds(pl.multiple_of(start, radius), width)
        elif dil == B_FOLD:
            qrows = pl.ds(r + dil * i0, tq, stride=dil)
            krows = pl.ds(r + dil * start, width, stride=dil)
        else:
            qrows = fold_rows(r, i0, tq)
            krows = fold_rows(r, start, width)
        if dil <= B_FOLD:
            q_src, k_src, v_src = q_refs[g], k_ref, v_ref
            og_dst, lse_dst = og_ref.at[g], lse_ref.at[g]
        else:
            q_src, k_src, v_src = fold_ref.at[0], fold_ref.at[1], fold_ref.at[2]
            og_dst, lse_dst = ofold_ref.at[0], ofold_ref.at[1]
        q = q_src[qrows, :].astype(BF16)
        k = k_src[krows, :].astype(BF16)
        v = v_src[krows, :].astype(BF16)
        variant = jnp.where(blk == 0, 0, jnp.where(blk == length // tq - 1, 2, 1))
        u = _dot_nt(q, k) + bias_ref[g, variant, :, 0:width]
        m = u.max(axis=-1, keepdims=True)
        p = jnp.exp2(u - m)
        den = p.sum(axis=-1, keepdims=True)
        og_dst[qrows, :] = _dot(p.astype(BF16), v) / den
        lse_dst[qrows, :] = jnp.broadcast_to(m + jnp.log2(den), (tq, LANES))

    for g, (window_len, dil) in enumerate(B_PAIRS):
        radius = window_len // (2 * dil)
        per_class = (seq // dil) // tq
        n_tiles = dil * per_class
        if dil > B_FOLD:
            for a, src in enumerate((q_refs[g], k_ref, v_ref)):
                for r in range(B_FOLD):
                    fold_ref[a, r * seg:(r + 1) * seg, :] = src[pl.ds(r, seg, stride=B_FOLD), :]

        def body(s, carry, g=g, dil=dil, radius=radius, per_class=per_class):
            for k in range(B_TILES_PER_STEP):
                t = s * B_TILES_PER_STEP + k
                tile(g, dil, radius, t // per_class, t % per_class)
            return carry

        lax.fori_loop(0, n_tiles // B_TILES_PER_STEP, body, 0)
        if dil > B_FOLD:
            for r in range(B_FOLD):
                og_ref[g, pl.ds(r, seg, stride=B_FOLD), :] = ofold_ref[0, r * seg:(r + 1) * seg, :]
                lse_ref[g, pl.ds(r, seg, stride=B_FOLD), :] = ofold_ref[1, r * seg:(r + 1) * seg, :]

    def merge(s, carry):
        rows = pl.ds(pl.multiple_of(s * tq, tq), tq)
        lse = [lse_ref[g, rows, :] for g in range(len(B_PAIRS))]
        top = functools.reduce(jnp.maximum, lse)
        wts = [jnp.exp2(x - top) for x in lse]
        num = sum(w * og_ref[g, rows, :] for g, w in enumerate(wts))
        o_ref[rows, :] = (num / sum(wts)).astype(o_ref.dtype)
        return carry

    lax.fori_loop(0, seq // tq, merge, 0, unroll=2)


def _attn_b(qkv, batch, seq, first=0):
    n_groups = len(B_PAIRS)
    assert all(seq % (d * B_TILE_Q) == 0 and (seq // d // B_TILE_Q * d) % B_TILES_PER_STEP == 0
               and d in (1, B_FOLD, B_FOLD * B_FOLD) for _, d in B_PAIRS)
    c = jnp.asarray(_alibi_slopes(N_MIX_HEADS)[A_Q_HEADS:] * np.float32(LOG2E))
    width_max = max(min(B_TILE_Q + 2 * (w // (2 * d)), seq // d) for w, d in B_PAIRS)

    def slab(head0):
        return pl.BlockSpec((None, seq, HEAD_DIM), lambda b, h: (first + head0 + h, b, 0))

    return pl.pallas_call(
        functools.partial(_attn_b_kernel, seq=seq),
        grid=(batch, B_HEADS),
        in_specs=[pl.BlockSpec(memory_space=pltpu.SMEM)]
        + [slab(g * B_HEADS) for g in range(n_groups)]
        + [slab(n_groups * B_HEADS), slab((n_groups + 1) * B_HEADS)],
        out_specs=pl.BlockSpec((seq, HEAD_DIM), lambda b, h: (b, h)),
        out_shape=jax.ShapeDtypeStruct((batch * seq, B_HEADS * HEAD_DIM), BF16),
        scratch_shapes=[pltpu.VMEM((n_groups, seq, HEAD_DIM), F32),
                        pltpu.VMEM((n_groups, seq, LANES), F32),
                        pltpu.VMEM((n_groups, 3, B_TILE_Q, width_max), F32),
                        pltpu.VMEM((3, seq, HEAD_DIM), F32),
                        pltpu.VMEM((2, seq, HEAD_DIM), F32)],
        compiler_params=pltpu.CompilerParams(
            dimension_semantics=("parallel", "arbitrary"), vmem_limit_bytes=VMEM_LIMIT_BYTES),
        name="attn_b",
    )(c, qkv, qkv, qkv, qkv, qkv)


C_POS_BITS = 6
C_SLOPE_PARTS = 3
C_AUG_POS = 2 * C_SLOPE_PARTS
C_STRIP = 32


def _attn_c_consts():
    c = (_alibi_slopes(C_HEADS).astype(np.float64) * math.log2(math.e)).astype(np.float32)
    parts, rest = [], c.copy()
    for _ in range(C_SLOPE_PARTS):
        p = rest.astype(BF16).astype(np.float32)
        parts.append(p)
        rest = rest - p
    qc = np.zeros((C_HEADS, 1, LANES), np.float32)
    kc = np.zeros((C_HEADS, 1, LANES), np.float32)
    for a, p in enumerate(parts):
        qc[:, 0, C_AUG_POS + 2 * a] = p
        qc[:, 0, C_AUG_POS + 2 * a + 1] = p
        kc[:, 0, 2 * a] = p * (1 << C_POS_BITS)
        kc[:, 0, 2 * a + 1] = p
    return jnp.asarray(c), jnp.asarray(qc), jnp.asarray(kc)


def _attn_c_kernel(c_ref, qc_ref, kc_ref, q_ref, k_ref, v_ref, lam_ref, sg_ref, o_ref,
                   kaug_ref, qv_ref, bias_ref, u_ref, p_ref, m_ref, l_ref, alpha_ref, acc_ref,
                   *, tq, tk, seq, lambda_init):
    h = pl.program_id(1)
    qi = pl.program_id(2)
    nk = seq // tk
    ratio = tk // tq
    kd = qi // ratio
    off = qi - kd * ratio
    lane = lax.broadcasted_iota(jnp.int32, (1, LANES), 1)
    hi_lane = (lane & 1) == 0
    lo_mask = (1 << C_POS_BITS) - 1

    @pl.when(qi == 0)
    def _():
        key_pos_lane = (lane >= C_AUG_POS) & (lane < 2 * C_AUG_POS)

        def fill(kb, carry):
            rows = pl.ds(pl.multiple_of(kb * tk, tk), tk)
            pos = kb * tk + lax.broadcasted_iota(jnp.int32, (tk, 1), 0)
            hi = (pos - (pos & lo_mask)).astype(F32)
            lo = (pos & lo_mask).astype(F32)
            aug = jnp.where(key_pos_lane, -jnp.where(hi_lane, hi, lo), kc_ref[0]).astype(BF16)
            for j in range(2):
                kaug_ref[j, rows, 0:HEAD_DIM] = k_ref[rows, j * HEAD_DIM:(j + 1) * HEAD_DIM]
                kaug_ref[j, rows, HEAD_DIM:2 * HEAD_DIM] = aug
            return carry

        lax.fori_loop(0, nk, fill, 0)
        jj = lax.broadcasted_iota(jnp.int32, (1, tk), 1)
        for o in range(ratio):
            ii = o * tq + lax.broadcasted_iota(jnp.int32, (tq, 1), 0)
            bias_ref[o] = c_ref[h] * jnp.abs(ii - jj).astype(F32)

    qpos = qi * tq + lax.broadcasted_iota(jnp.int32, (tq, 1), 0)
    q_hi = (qpos >> C_POS_BITS).astype(F32)
    q_lo = (qpos & lo_mask).astype(F32)
    q_aug = jnp.where(lane < C_AUG_POS, jnp.where(hi_lane, q_hi, q_lo), qc_ref[0])
    for j in range(2):
        qj = q_ref[:, j * HEAD_DIM:(j + 1) * HEAD_DIM]
        for variant, aug in enumerate((-q_aug, q_aug, jnp.zeros_like(q_aug))):
            qv_ref[variant, j, :, 0:HEAD_DIM] = qj
            qv_ref[variant, j, :, HEAD_DIM:2 * HEAD_DIM] = aug.astype(BF16)

    def key_rows(step):
        kb = kd + step
        kb = jnp.where(kb >= nk, kb - nk, kb)
        return kb, pl.ds(pl.multiple_of(kb * tk, tk), tk)

    def scores(step, slot):
        kb, rows = key_rows(step)
        variant = 2 if step == 0 else jnp.where(kb < kd, 0, 1)
        for j in range(2):
            u_ref[slot, j] = _dot_nt(qv_ref[variant, j], kaug_ref[j, rows, :])

    def accumulate(step, slot):
        _, rows = key_rows(step)
        vblk = v_ref[rows, :]
        for j in range(2):
            for r0 in range(0, tq, C_STRIP):
                rs = slice(r0, r0 + C_STRIP)
                u = u_ref[slot, j, rs, :]
                if step == 0:
                    u = u - bias_ref[off, rs, :]
                    m_new = jnp.broadcast_to(jnp.max(u, axis=1, keepdims=True), (C_STRIP, LANES))
                else:
                    m_prev = m_ref[j, rs, :]
                    m_new = jnp.maximum(m_prev, jnp.max(u, axis=1, keepdims=True))
                    alpha = jnp.exp2(m_prev - m_new)
                    alpha_ref[j, rs, :] = alpha
                p = jnp.exp2(u - jnp.tile(m_new, (1, tk // LANES)))
                psum = p[:, 0:LANES]
                for cb in range(1, tk // LANES):
                    psum = psum + p[:, cb * LANES:(cb + 1) * LANES]
                l_ref[j, rs, :] = psum if step == 0 else alpha * l_ref[j, rs, :] + psum
                m_ref[j, rs, :] = m_new
                p_ref[slot, j, rs, :] = p.astype(BF16)
            pv = _dot(p_ref[slot, j], vblk)
            acc_ref[j] = pv if step == 0 else jnp.tile(alpha_ref[j], (1, 2)) * acc_ref[j] + pv

    scores(0, 0)
    for step in range(nk):
        if step + 1 < nk:
            scores(step + 1, (step + 1) % 2)
        accumulate(step, step % 2)

    lp = lam_ref[...]
    lam = (jnp.exp(jnp.sum(lp[0:1] * lp[1:2], axis=-1, keepdims=True))
           - jnp.exp(jnp.sum(lp[2:3] * lp[3:4], axis=-1, keepdims=True)) + lambda_init)
    den0 = jnp.sum(l_ref[0], axis=1, keepdims=True)
    den1 = jnp.sum(l_ref[1], axis=1, keepdims=True)
    o = acc_ref[0] / den0 - lam * (acc_ref[1] / den1)
    o = (_rms(o) * sg_ref[...]) * (1.0 - lambda_init)
    o_ref[...] = o.astype(o_ref.dtype)


def _attn_c(proj, lam_params, subln_g, wl, lambda_init, batch, seq, tq=512, tk=1024):
    wh = 2 * HEAD_DIM
    d = C_HEADS * wh
    tk = min(tk, seq)
    tq = min(tq, tk)
    nq = seq // tq
    assert seq <= (1 << (2 * C_POS_BITS)) and tk % tq == 0 and tq % C_STRIP == 0 and tk % LANES == 0
    c, qc, kc = _attn_c_consts()
    const_spec = pl.BlockSpec((1, 1, LANES), lambda b, h, qi: (h, 0, 0))
    return pl.pallas_call(
        functools.partial(_attn_c_kernel, tq=tq, tk=tk, seq=seq, lambda_init=lambda_init),
        grid=(batch, C_HEADS, nq),
        in_specs=[
            pl.BlockSpec(memory_space=pltpu.SMEM),
            const_spec, const_spec,
            pl.BlockSpec((tq, wh), lambda b, h, qi: (b * nq + qi, h)),
            pl.BlockSpec((seq, wh), lambda b, h, qi: (b, C_HEADS + h)),
            pl.BlockSpec((seq, wh), lambda b, h, qi: (b, 2 * C_HEADS + h)),
            pl.BlockSpec((None, 4, HEAD_DIM), lambda b, h, qi: (wl, 0, 0)),
            pl.BlockSpec((None, 1, wh), lambda b, h, qi: (wl, 0, 0)),
        ],
        out_specs=pl.BlockSpec((tq, wh), lambda b, h, qi: (b * nq + qi, h)),
        out_shape=jax.ShapeDtypeStruct((batch * seq, d), BF16),
        scratch_shapes=[
            pltpu.VMEM((2, seq, wh), BF16),
            pltpu.VMEM((3, 2, tq, wh), BF16),
            pltpu.VMEM((tk // tq, tq, tk), F32),
            pltpu.VMEM((2, 2, tq, tk), F32),
            pltpu.VMEM((2, 2, tq, tk), BF16),
            pltpu.VMEM((2, tq, LANES), F32),
            pltpu.VMEM((2, tq, LANES), F32),
            pltpu.VMEM((2, tq, LANES), F32),
            pltpu.VMEM((2, tq, wh), F32),
        ],
        compiler_params=pltpu.CompilerParams(
            dimension_semantics=("parallel", "parallel", "arbitrary"),
            vmem_limit_bytes=VMEM_LIMIT_BYTES),
        name="attn_c",
    )(c, qc, kc, proj, proj, proj, lam_params, subln_g.reshape(-1, 1, wh))


def kernel(x, c, ada_w, ada_b, norm_g, ab_w_in, ab_w_out, a_sink, c_w_in, c_w_out, c_lambda,
           c_subln_g, ffn_w_up, ffn_conv_w, ffn_conv_b, ffn_w_down):
    batch, seq, d = x.shape
    depth = ada_w.shape[0]
    dff = ffn_w_down.shape[1]
    dff_pad = -(-dff // FFN_TF) * FFN_TF

    ab_w_in_b, ab_w_out_b = ab_w_in.astype(BF16), ab_w_out.astype(BF16)
    c_w_in_b, c_w_out_b = c_w_in.astype(BF16), c_w_out.astype(BF16)
    w_gate_up = _prep_w_up(ffn_w_up, dff_pad)
    w_down = _prep_w_down(ffn_w_down, dff_pad)
    conv_w = jnp.pad(ffn_conv_w, ((0, 0), (0, 0), (0, dff_pad - dff)))
    conv_b = jnp.pad(ffn_conv_b, ((0, 0), (0, dff_pad - dff))).reshape(depth, 1, dff_pad)

    q_scale = np.float32(HEAD_DIM ** -0.5 * LOG2E)
    ab_scale = np.ones((ab_w_in.shape[2],), np.float32)
    wq, wkv_a = A_Q_HEADS * HEAD_DIM, 2 * A_KV_HEADS * HEAD_DIM
    ab_scale[:wq] = q_scale
    ab_scale[wq + wkv_a:wq + wkv_a + len(B_PAIRS) * B_HEADS * HEAD_DIM] = q_scale
    ab_col_scale = jnp.asarray(ab_scale)
    c_col_scale = jnp.asarray(np.concatenate([np.full((d,), q_scale, np.float32),
                                              np.ones((2 * d,), np.float32)]))

    mod = _ada(c, ada_w, ada_b)
    xs = x.reshape(batch * seq, d)
    for layer in range(depth):
        wl = layer // 2
        if layer % 2 == 0:
            proj, slabs = _norm_proj(xs, norm_g, mod, layer, ab_w_in_b, wl, ab_col_scale, seq,
                                     tm=AB_PROJ_TM, tn=AB_PROJ_TN, slab_from=0)
            mixed = (_attn_a(proj, a_sink, wl, batch, seq),
                     _attn_b(slabs, batch, seq, first=A_Q_HEADS + 2 * A_KV_HEADS))
            xs = _out_proj(mixed, ab_w_out_b, wl, xs, norm_g, mod, layer, seq)
        else:
            lambda_init = 0.8 - 0.6 * math.exp(-0.3 * layer)
            proj = _norm_proj(xs, norm_g, mod, layer, c_w_in_b, wl, c_col_scale, seq, tn=C_PROJ_TN)
            mixed = _attn_c(proj, c_lambda, c_subln_g, wl, lambda_init, batch, seq)
            xs = _out_proj((mixed,), c_w_out_b, wl, xs, norm_g, mod, layer, seq)
        xs = _ffn(xs, norm_g, mod, layer, w_gate_up, conv_w, conv_b, w_down, seq)
    return xs.reshape(batch, seq, d)
```

```python
import functools
import math

import numpy as np
import jax
import jax.numpy as jnp
from jax import lax
from jax.experimental import pallas as pl
from jax.experimental.pallas import tpu as pltpu

F32 = jnp.float32
BF16 = jnp.bfloat16

HEAD_DIM = 128
A_Q_HEADS = 4
A_KV_HEADS = 2
A_GROUP = A_Q_HEADS // A_KV_HEADS
A_RADIUS = 128
B_PAIRS = ((128, 1), (512, 4), (2048, 16))
B_HEADS = 4
N_MIX_HEADS = A_Q_HEADS + len(B_PAIRS) * B_HEADS
C_HEADS = 8
CONV_WIDTH = 3
EPS = 1e-6
NEG_INF = -1e30

SUBLANES_F32 = 8
SUBLANES_BF16 = 16
LANES = 128
VMEM_LIMIT_BYTES = 56 * 1024 * 1024

AB_BLOCK_Q = 256
FFN_HALO = SUBLANES_BF16
FFN_TF = 512
AB_PROJ_TM = 512
AB_PROJ_TN = 1792
C_PROJ_TN = 1536
LOG2E = math.log2(math.e)


def _alibi_slopes(n):
    return (2.0 ** (-8.0 * np.arange(1, n + 1, dtype=np.float32) / n)).astype(np.float32)


def _rms(x):
    return x * lax.rsqrt(jnp.mean(x * x, axis=-1, keepdims=True) + EPS)


def _dot(a, b):
    return jnp.dot(a, b, preferred_element_type=F32)


def _dot_nt(a, b):
    return lax.dot_general(a, b, (((1,), (1,)), ((), ())), preferred_element_type=F32)


def _ada_kernel(c_ref, w_ref, b_ref, o_ref):
    batch = c_ref.shape[0]
    tn = w_ref.shape[1]
    for b in range(batch):
        cb = c_ref[b]
        cond = cb * jax.nn.sigmoid(cb)
        prod = w_ref[...] * jnp.tile(cond, (1, tn // LANES))
        o_ref[b:b + 1, :] = jnp.sum(prod, axis=0, keepdims=True) + b_ref[...]
    o_ref[batch:, :] = jnp.zeros((o_ref.shape[0] - batch, tn), F32)


def _ada(c, ada_w, ada_b, tn=1024):
    depth, d, n = ada_w.shape
    b = c.shape[0]
    assert b < SUBLANES_F32 and n % d == 0 and d % tn == 0
    per_chunk = d // tn
    c_lanes = jnp.broadcast_to(c[:, :, None], (b, d, LANES))
    return pl.pallas_call(
        _ada_kernel,
        grid=(depth, n // tn),
        in_specs=[
            pl.BlockSpec((b, d, LANES), lambda l, j: (0, 0, 0)),
            pl.BlockSpec((None, d, tn), lambda l, j: (l, 0, j)),
            pl.BlockSpec((None, 1, tn), lambda l, j: (l, 0, j)),
        ],
        out_specs=pl.BlockSpec((None, None, SUBLANES_F32, tn),
                               lambda l, j: (l, j // per_chunk, 0, j % per_chunk)),
        out_shape=jax.ShapeDtypeStruct((depth, n // d, SUBLANES_F32, d), F32),
        compiler_params=pltpu.CompilerParams(
            dimension_semantics=("arbitrary", "arbitrary"), vmem_limit_bytes=VMEM_LIMIT_BYTES),
        name="ada",
    )(c_lanes, ada_w, ada_b.reshape(depth, 1, n))


MOD_SH1, MOD_SC1, MOD_G1, MOD_SH2, MOD_SC2, MOD_G2 = range(6)
NORM_PRE_MIX, NORM_POST_MIX, NORM_PRE_FFN, NORM_POST_FFN = range(4)


def _mod_spec(layer, chunk, d):
    return pl.BlockSpec((None, None, SUBLANES_F32, d), lambda *_: (layer, chunk, 0, 0))


def _norm_spec(layer, d):
    return pl.BlockSpec((None, 4, d), lambda *_: (layer, 0, 0))


def _batch_row(ref, b):
    return ref[pl.ds(b, 1), :]


NORM_STRIP = 32
NORM_UNROLL = 8


def _strip_rows(s):
    return pl.ds(pl.multiple_of(s * NORM_STRIP, NORM_STRIP), NORM_STRIP)


def _row_rsqrt_ms(src_ref, rs_ref):
    n_rows, d = src_ref.shape

    def body(s, carry):
        rows = _strip_rows(s)
        x = src_ref[rows, :]
        ms = jnp.sum(x * x, axis=1, keepdims=True) * (1.0 / d)
        rs_ref[rows, :] = jnp.broadcast_to(lax.rsqrt(ms + EPS), (NORM_STRIP, LANES))
        return carry

    lax.fori_loop(0, n_rows // NORM_STRIP, body, 0, unroll=NORM_UNROLL)


def _scaled_rows(src_ref, rs_ref, rows):
    return src_ref[rows, :] * jnp.tile(rs_ref[rows, :], (1, src_ref.shape[1] // LANES))


def _cast_pad_kernel(w_ref, o_ref, *, axis):
    n = w_ref.shape[axis]
    if axis == 0:
        o_ref[:n, :] = w_ref[...].astype(o_ref.dtype)
        o_ref[n:, :] = jnp.zeros((o_ref.shape[0] - n, o_ref.shape[1]), o_ref.dtype)
    else:
        o_ref[:, :n] = w_ref[...].astype(o_ref.dtype)
        o_ref[:, n:] = jnp.zeros((o_ref.shape[0], o_ref.shape[1] - n), o_ref.dtype)


def _prep_w_up(w_up, dff_pad, layers, tr=256):
    _, d, two_dff = w_up.shape
    dff = two_dff // 2
    return pl.pallas_call(
        functools.partial(_cast_pad_kernel, axis=1),
        grid=(layers, 2, d // tr),
        in_specs=[pl.BlockSpec((None, tr, dff), lambda l, p, r: (l, r, p))],
        out_specs=pl.BlockSpec((None, None, tr, dff_pad), lambda l, p, r: (l, p, r, 0)),
        out_shape=jax.ShapeDtypeStruct((layers, 2, d, dff_pad), BF16),
        compiler_params=pltpu.CompilerParams(
            dimension_semantics=("arbitrary",) * 3, vmem_limit_bytes=VMEM_LIMIT_BYTES),
        name="prep_w_up",
    )(w_up)


def _prep_w_down(w_down, dff_pad, layers, tc=256):
    _, dff, d = w_down.shape
    return pl.pallas_call(
        functools.partial(_cast_pad_kernel, axis=0),
        grid=(layers, d // tc),
        in_specs=[pl.BlockSpec((None, dff, tc), lambda l, c: (l, 0, c))],
        out_specs=pl.BlockSpec((None, dff_pad, tc), lambda l, c: (l, 0, c)),
        out_shape=jax.ShapeDtypeStruct((layers, dff_pad, d), BF16),
        compiler_params=pltpu.CompilerParams(
            dimension_semantics=("arbitrary",) * 2, vmem_limit_bytes=VMEM_LIMIT_BYTES),
        name="prep_w_down",
    )(w_down)


def _norm_proj_kernel(x_ref, g_ref, sc_ref, sh_ref, w_ref, cs_ref, o_ref, *rest, per_seq, slab_from):
    if slab_from is None:
        h_ref, rs_ref = rest
    else:
        slab_ref, h_ref, rs_ref = rest
    j = pl.program_id(1)

    @pl.when(j == 0)
    def _():
        b = pl.program_id(0) // per_seq
        g_mod = g_ref[NORM_PRE_MIX:NORM_PRE_MIX + 1, :] * (1.0 + _batch_row(sc_ref, b))
        shift = _batch_row(sh_ref, b)
        _row_rsqrt_ms(x_ref, rs_ref)

        def body(s, carry):
            rows = _strip_rows(s)
            h_ref[rows, :] = (_scaled_rows(x_ref, rs_ref, rows) * g_mod + shift).astype(BF16)
            return carry

        lax.fori_loop(0, x_ref.shape[0] // NORM_STRIP, body, 0, unroll=NORM_UNROLL)

    res = _dot(h_ref[...], w_ref[...]) * cs_ref[...]
    o_ref[...] = res.astype(o_ref.dtype)
    if slab_from is not None:
        @pl.when(j >= slab_from)
        def _():
            for hh in range(slab_ref.shape[0]):
                slab_ref[hh] = res[:, hh * LANES:(hh + 1) * LANES]


def _norm_proj(x, norm_g, mod, layer, w, wl, col_scale, seq, tm=1024, tn=512, slab_from=None):
    m, d = x.shape
    n = w.shape[2]
    tm = min(tm, seq)
    nj = n // tn
    out_specs = pl.BlockSpec((tm, tn), lambda i, j: (i, j))
    out_shape = jax.ShapeDtypeStruct((m, n), BF16)
    if slab_from is not None:
        per_block = tn // LANES
        out_specs = [out_specs, pl.BlockSpec((per_block, tm, LANES),
                                             lambda i, j: (jnp.maximum(j - slab_from, 0), i, 0))]
        out_shape = [out_shape, jax.ShapeDtypeStruct(((nj - slab_from) * per_block, m, LANES), F32)]
    return pl.pallas_call(
        functools.partial(_norm_proj_kernel, per_seq=seq // tm, slab_from=slab_from),
        grid=(m // tm, nj),
        in_specs=[
            pl.BlockSpec((tm, d), lambda i, j: (i, 0)),
            _norm_spec(layer, d),
            _mod_spec(layer, MOD_SC1, d),
            _mod_spec(layer, MOD_SH1, d),
            pl.BlockSpec((None, d, tn), lambda i, j: (wl, 0, j)),
            pl.BlockSpec((1, tn), lambda i, j: (0, j)),
        ],
        out_specs=out_specs,
        out_shape=out_shape,
        scratch_shapes=[pltpu.VMEM((tm, d), BF16), pltpu.VMEM((tm, LANES), F32)],
        compiler_params=pltpu.CompilerParams(
            dimension_semantics=("parallel", "arbitrary"), vmem_limit_bytes=VMEM_LIMIT_BYTES),
        name="norm_proj",
    )(x, norm_g, mod, mod, w, col_scale.reshape(1, n))


def _out_proj_kernel(*refs, per_seq, widths):
    a_refs = refs[:len(widths)]
    w_ref, x_ref, g_ref, gate_ref, o_ref = refs[len(widths):]
    b = pl.program_id(0) // per_seq
    y, k0 = None, 0
    for a_ref, k in zip(a_refs, widths):
        part = _dot(a_ref[...], w_ref[k0:k0 + k, :])
        y = part if y is None else y + part
        k0 += k
    g = g_ref[NORM_POST_MIX:NORM_POST_MIX + 1, :]
    o_ref[...] = x_ref[...] + _batch_row(gate_ref, b) * (_rms(y) * g)


def _out_proj(mixed, w, wl, x, norm_g, mod, layer, seq, tm=512):
    m = x.shape[0]
    widths = tuple(a.shape[1] for a in mixed)
    k, d = w.shape[1], w.shape[2]
    assert sum(widths) == k
    tm = min(tm, seq)
    return pl.pallas_call(
        functools.partial(_out_proj_kernel, per_seq=seq // tm, widths=widths),
        grid=(m // tm,),
        in_specs=[pl.BlockSpec((tm, kk), lambda i: (i, 0)) for kk in widths] + [
            pl.BlockSpec((None, k, d), lambda i: (wl, 0, 0)),
            pl.BlockSpec((tm, d), lambda i: (i, 0)),
            _norm_spec(layer, d),
            _mod_spec(layer, MOD_G1, d),
        ],
        out_specs=pl.BlockSpec((tm, d), lambda i: (i, 0)),
        out_shape=jax.ShapeDtypeStruct((m, d), F32),
        compiler_params=pltpu.CompilerParams(
            dimension_semantics=("parallel",), vmem_limit_bytes=VMEM_LIMIT_BYTES),
        name="out_proj",
    )(*mixed, w, x, norm_g, mod)


def _ffn_kernel(x_ref, xp_ref, xn_ref, g_ref, sc_ref, sh_ref, wg_ref, wu_ref, cw_ref, cb_ref,
                wd_ref, gate_ref, *rest, tm, per_seq, prep_next, down_chunks):
    i = pl.program_id(0)
    f = pl.program_id(1)
    b = i // per_seq
    halo = FFN_HALO
    d = x_ref.shape[1]
    if prep_next:
        ng_ref, nu_ref, nd_ref, o_ref, ogu_ref, owd_ref, h_ref, rs_ref = rest
    else:
        o_ref, h_ref, rs_ref = rest

    @pl.when(f == 0)
    def _():
        g_mod = g_ref[NORM_PRE_FFN:NORM_PRE_FFN + 1, :] * (1.0 + _batch_row(sc_ref, b))
        shift = _batch_row(sh_ref, b)

        keep_prev = jnp.where(i % per_seq == 0, 0.0, 1.0)
        keep_next = jnp.where(i % per_seq == per_seq - 1, 0.0, 1.0)
        zeros = jnp.zeros((halo - SUBLANES_F32, d), F32)
        hp = (_rms(xp_ref[...]) * g_mod + shift) * keep_prev
        hn = (_rms(xn_ref[...]) * g_mod + shift) * keep_next
        h_ref[0:halo, :] = jnp.concatenate([zeros, hp], axis=0).astype(BF16)
        h_ref[halo + tm:, :] = jnp.concatenate([hn, zeros], axis=0).astype(BF16)

        _row_rsqrt_ms(x_ref, rs_ref)

        def body(s, carry):
            rows = _strip_rows(s)
            dst = pl.ds(pl.multiple_of(halo + s * NORM_STRIP, SUBLANES_BF16), NORM_STRIP)
            h_ref[dst, :] = (_scaled_rows(x_ref, rs_ref, rows) * g_mod + shift).astype(BF16)
            return carry

        lax.fori_loop(0, tm // NORM_STRIP, body, 0, unroll=NORM_UNROLL)
        o_ref[...] = jnp.zeros_like(o_ref)

    rows = tm + 2 * halo
    gate = _dot(h_ref[...], wg_ref[...])
    up = _dot(h_ref[halo:halo + tm, :], wu_ref[...])
    g_prev = pltpu.roll(gate, 1, 0)[halo:halo + tm]
    g_next = pltpu.roll(gate, rows - 1, 0)[halo:halo + tm]
    g_mid = gate[halo:halo + tm]
    cw = cw_ref[...]
    z = g_prev * cw[0:1] + g_mid * cw[1:2] + g_next * cw[2:3] + cb_ref[...]
    act = jax.nn.gelu(z, approximate=True) * up
    o_ref[...] += _dot(act.astype(BF16), wd_ref[...])

    if prep_next:
        dff = ng_ref.shape[1]
        for half, src in enumerate((ng_ref, nu_ref)):
            ogu_ref[half, :, 0:dff] = src[...].astype(BF16)
            ogu_ref[half, :, dff:] = jnp.zeros((src.shape[0], ogu_ref.shape[2] - dff), BF16)
        is_data = i * pl.num_programs(1) + f < down_chunks
        owd_ref[...] = jnp.where(is_data, nd_ref[...], 0.0).astype(BF16)

    @pl.when(f == pl.num_programs(1) - 1)
    def _():
        g_gate = g_ref[NORM_POST_FFN:NORM_POST_FFN + 1, :] * _batch_row(gate_ref, b)
        _row_rsqrt_ms(o_ref, rs_ref)

        def body(s, carry):
            rows = _strip_rows(s)
            o_ref[rows, :] = x_ref[rows, :] + _scaled_rows(o_ref, rs_ref, rows) * g_gate
            return carry

        lax.fori_loop(0, tm // NORM_STRIP, body, 0, unroll=NORM_UNROLL)


FFN_PREP_UP_ROWS = 32
FFN_PREP_DOWN_ROWS = 128


def _ffn(x, norm_g, mod, layer, w_gate_up, conv_w, conv_b, w_down, seq, raw_next=None,
         tm=1024, tf=FFN_TF):
    m, d = x.shape
    dff_pad = w_down.shape[1]
    tm = min(tm, seq)
    per_seq = seq // tm
    nblk8 = m // SUBLANES_F32
    r8 = tm // SUBLANES_F32
    nf = dff_pad // tf
    prep_specs, prep_args, prep_out_specs, prep_out_shapes, down_chunks = [], [], [], [], None
    if raw_next is not None:
        raw_up, raw_down = raw_next
        dff = raw_down.shape[1]
        steps = (m // tm) * nf
        up_rows = FFN_PREP_UP_ROWS
        while d // up_rows > steps:
            up_rows *= 2
        up_chunks = d // up_rows
        down_chunks = dff // FFN_PREP_DOWN_ROWS
        pad_chunks = (dff_pad - dff) // FFN_PREP_DOWN_ROWS
        assert d % up_rows == 0 and dff % FFN_PREP_DOWN_ROWS == 0
        assert pad_chunks * FFN_PREP_DOWN_ROWS == dff_pad - dff and pad_chunks <= 1
        assert down_chunks + pad_chunks <= steps
        nxt = layer + 1

        def up_chunk(i, f):
            return jnp.minimum(i * nf + f, up_chunks - 1)

        prep_specs = [
            pl.BlockSpec((None, up_rows, dff), lambda i, f: (nxt, up_chunk(i, f), 0)),
            pl.BlockSpec((None, up_rows, dff), lambda i, f: (nxt, up_chunk(i, f), 1)),
            pl.BlockSpec((None, FFN_PREP_DOWN_ROWS, d),
                         lambda i, f: (nxt, jnp.minimum(i * nf + f, down_chunks - 1), 0)),
        ]
        prep_args = [raw_up, raw_up, raw_down]
        prep_out_specs = [
            pl.BlockSpec((2, up_rows, dff_pad), lambda i, f: (0, up_chunk(i, f), 0)),
            pl.BlockSpec((FFN_PREP_DOWN_ROWS, d),
                         lambda i, f: (jnp.minimum(i * nf + f, down_chunks + pad_chunks - 1), 0)),
        ]
        prep_out_shapes = [jax.ShapeDtypeStruct((2, d, dff_pad), BF16),
                           jax.ShapeDtypeStruct((dff_pad, d), BF16)]
    x_out_spec = pl.BlockSpec((tm, d), lambda i, f: (i, 0), pipeline_mode=pl.Buffered(1))
    x_out_shape = jax.ShapeDtypeStruct((m, d), F32)
    outs = pl.pallas_call(
        functools.partial(_ffn_kernel, tm=tm, per_seq=per_seq, prep_next=raw_next is not None,
                          down_chunks=down_chunks),
        grid=(m // tm, nf),
        in_specs=[
            pl.BlockSpec((tm, d), lambda i, f: (i, 0)),
            pl.BlockSpec((SUBLANES_F32, d), lambda i, f: (jnp.maximum(i * r8 - 1, 0), 0)),
            pl.BlockSpec((SUBLANES_F32, d), lambda i, f: (jnp.minimum((i + 1) * r8, nblk8 - 1), 0)),
            _norm_spec(layer, d),
            _mod_spec(layer, MOD_SC2, d),
            _mod_spec(layer, MOD_SH2, d),
            pl.BlockSpec((None, None, d, tf), lambda i, f: (0, 0, 0, f)),
            pl.BlockSpec((None, None, d, tf), lambda i, f: (0, 1, 0, f)),
            pl.BlockSpec((None, CONV_WIDTH, tf), lambda i, f: (layer, 0, f)),
            pl.BlockSpec((None, 1, tf), lambda i, f: (layer, 0, f)),
            pl.BlockSpec((None, tf, d), lambda i, f: (0, f, 0)),
            _mod_spec(layer, MOD_G2, d),
        ] + prep_specs,
        out_specs=[x_out_spec] + prep_out_specs,
        out_shape=[x_out_shape] + prep_out_shapes,
        scratch_shapes=[pltpu.VMEM((tm + 2 * FFN_HALO, d), BF16), pltpu.VMEM((tm, LANES), F32)],
        compiler_params=pltpu.CompilerParams(
            dimension_semantics=("arbitrary", "arbitrary"), vmem_limit_bytes=VMEM_LIMIT_BYTES),
        name="ffn",
    )(x, x, x, norm_g, mod, mod, w_gate_up, w_gate_up, conv_w, conv_b, w_down, mod, *prep_args)
    if raw_next is None:
        return outs[0], None, None
    return outs[0], outs[1][None], outs[2][None]


def _attn_a_kernel(c_ref, sink_ref, qa_ref, ka_ref, va_ref, o_ref, *, seq, wl):
    tq = AB_BLOCK_Q
    t0 = pl.program_id(1) * tq
    width = min(tq + 2 * A_RADIUS, seq)
    start = pl.multiple_of(jnp.clip(t0 - A_RADIUS, 0, seq - width), SUBLANES_BF16)
    qpos = t0 + lax.broadcasted_iota(jnp.int32, (tq, 1), 0)
    kpos = start + lax.broadcasted_iota(jnp.int32, (1, width), 1)
    dist = jnp.abs(qpos - kpos)
    valid = dist <= A_RADIUS
    distf = dist.astype(F32)
    for kvh in range(A_KV_HEADS):
        cols = slice(kvh * HEAD_DIM, (kvh + 1) * HEAD_DIM)
        kw = ka_ref[pl.ds(start, width), cols]
        vw = va_ref[pl.ds(start, width), cols]
        for gq in range(A_GROUP):
            hq = kvh * A_GROUP + gq
            u = _dot_nt(qa_ref[:, hq * HEAD_DIM:(hq + 1) * HEAD_DIM], kw)
            u = jnp.where(valid, u - c_ref[hq] * distf, NEG_INF)
            sink = sink_ref[wl, hq] * LOG2E
            m = jnp.maximum(u.max(axis=-1, keepdims=True), sink)
            p = jnp.exp2(u - m)
            den = p.sum(axis=-1, keepdims=True) + jnp.exp2(sink - m)
            o = _dot(p.astype(BF16), vw) / den
            o_ref[:, hq * HEAD_DIM:(hq + 1) * HEAD_DIM] = o.astype(o_ref.dtype)


def _attn_a(proj, sink, wl, batch, seq):
    tq = min(AB_BLOCK_Q, seq)
    nq = seq // tq
    wq = A_Q_HEADS * HEAD_DIM
    wka = A_KV_HEADS * HEAD_DIM
    assert wq % wka == 0
    c = jnp.asarray(_alibi_slopes(N_MIX_HEADS)[:A_Q_HEADS] * np.float32(LOG2E))
    smem = pl.BlockSpec(memory_space=pltpu.SMEM)
    return pl.pallas_call(
        functools.partial(_attn_a_kernel, seq=seq, wl=wl),
        grid=(batch, nq),
        in_specs=[
            smem, smem,
            pl.BlockSpec((tq, wq), lambda b, n: (b * nq + n, 0)),
            pl.BlockSpec((seq, wka), lambda b, n: (b, wq // wka)),
            pl.BlockSpec((seq, wka), lambda b, n: (b, wq // wka + 1)),
        ],
        out_specs=pl.BlockSpec((tq, wq), lambda b, n: (b * nq + n, 0)),
        out_shape=jax.ShapeDtypeStruct((batch * seq, wq), BF16),
        compiler_params=pltpu.CompilerParams(
            dimension_semantics=("parallel", "arbitrary"), vmem_limit_bytes=VMEM_LIMIT_BYTES),
        name="attn_a",
    )(c, sink, proj, proj, proj)


B_TILE_Q = 128
B_FOLD = 4
B_TILES_PER_STEP = 16


def _attn_b_kernel(c_ref, q0_ref, q1_ref, q2_ref, k_ref, v_ref, o_ref, og_ref, lse_ref, bias_ref,
                   fold_ref, ofold_ref, *, seq):
    hb = pl.program_id(1)
    tq = B_TILE_Q
    q_refs = (q0_ref, q1_ref, q2_ref)
    seg = seq // B_FOLD

    for g, (window_len, dil) in enumerate(B_PAIRS):
        radius = window_len // (2 * dil)
        width = min(tq + 2 * radius, seq // dil)
        slope = c_ref[g * B_HEADS + hb] * float(dil)
        for variant, offset in enumerate((0, radius, width - tq)):
            dist = jnp.abs((offset + lax.broadcasted_iota(jnp.int32, (tq, 1), 0))
                           - lax.broadcasted_iota(jnp.int32, (1, width), 1))
            bias_ref[g, variant, :, 0:width] = jnp.where(dist <= radius,
                                                         -slope * dist.astype(F32), NEG_INF)

    def fold_rows(r, first, n):
        return pl.ds((r % B_FOLD) * seg + r // B_FOLD + B_FOLD * first, n, stride=B_FOLD)

    def tile(g, dil, radius, r, blk):
        length = seq // dil
        width = min(tq + 2 * radius, length)
        i0 = blk * tq
        start = jnp.clip(i0 - radius, 0, length - width)
        if dil == 1:
            qrows = pl.ds(pl.multiple_of(i0, tq), tq)
            krows = pl.ds(pl.multiple_of(start, radius), width)
        elif dil == B_FOLD:
            qrows = pl.ds(r + dil * i0, tq, stride=dil)
            krows = pl.ds(r + dil * start, width, stride=dil)
        else:
            qrows = fold_rows(r, i0, tq)
            krows = fold_rows(r, start, width)
        if dil <= B_FOLD:
            q_src, k_src, v_src = q_refs[g], k_ref, v_ref
            og_dst, lse_dst = og_ref.at[g], lse_ref.at[g]
        else:
            q_src, k_src, v_src = fold_ref.at[0], fold_ref.at[1], fold_ref.at[2]
            og_dst, lse_dst = ofold_ref.at[0], ofold_ref.at[1]
        q = q_src[qrows, :].astype(BF16)
        k = k_src[krows, :].astype(BF16)
        v = v_src[krows, :].astype(BF16)
        variant = jnp.where(blk == 0, 0, jnp.where(blk == length // tq - 1, 2, 1))
        u = _dot_nt(q, k) + bias_ref[g, variant, :, 0:width]
        m = u.max(axis=-1, keepdims=True)
        p = jnp.exp2(u - m)
        den = p.sum(axis=-1, keepdims=True)
        og_dst[qrows, :] = _dot(p.astype(BF16), v) / den
        lse_dst[qrows, :] = jnp.broadcast_to(m + jnp.log2(den), (tq, LANES))

    for g, (window_len, dil) in enumerate(B_PAIRS):
        radius = window_len // (2 * dil)
        per_class = (seq // dil) // tq
        n_tiles = dil * per_class
        if dil > B_FOLD:
            for a, src in enumerate((q_refs[g], k_ref, v_ref)):
                for r in range(B_FOLD):
                    fold_ref[a, r * seg:(r + 1) * seg, :] = src[pl.ds(r, seg, stride=B_FOLD), :]

        def body(s, carry, g=g, dil=dil, radius=radius, per_class=per_class):
            for k in range(B_TILES_PER_STEP):
                t = s * B_TILES_PER_STEP + k
                tile(g, dil, radius, t // per_class, t % per_class)
            return carry

        lax.fori_loop(0, n_tiles // B_TILES_PER_STEP, body, 0)
        if dil > B_FOLD:
            for r in range(B_FOLD):
                og_ref[g, pl.ds(r, seg, stride=B_FOLD), :] = ofold_ref[0, r * seg:(r + 1) * seg, :]
                lse_ref[g, pl.ds(r, seg, stride=B_FOLD), :] = ofold_ref[1, r * seg:(r + 1) * seg, :]

    def merge(s, carry):
        rows = pl.ds(pl.multiple_of(s * tq, tq), tq)
        lse = [lse_ref[g, rows, :] for g in range(len(B_PAIRS))]
        top = functools.reduce(jnp.maximum, lse)
        wts = [jnp.exp2(x - top) for x in lse]
        num = sum(w * og_ref[g, rows, :] for g, w in enumerate(wts))
        o_ref[rows, :] = (num / sum(wts)).astype(o_ref.dtype)
        return carry

    lax.fori_loop(0, seq // tq, merge, 0, unroll=2)


def _attn_b(qkv, batch, seq, first=0):
    n_groups = len(B_PAIRS)
    assert all(seq % (d * B_TILE_Q) == 0 and (seq // d // B_TILE_Q * d) % B_TILES_PER_STEP == 0
               and d in (1, B_FOLD, B_FOLD * B_FOLD) for _, d in B_PAIRS)
    c = jnp.asarray(_alibi_slopes(N_MIX_HEADS)[A_Q_HEADS:] * np.float32(LOG2E))
    width_max = max(min(B_TILE_Q + 2 * (w // (2 * d)), seq // d) for w, d in B_PAIRS)

    def slab(head0):
        return pl.BlockSpec((None, seq, HEAD_DIM), lambda b, h: (first + head0 + h, b, 0))

    return pl.pallas_call(
        functools.partial(_attn_b_kernel, seq=seq),
        grid=(batch, B_HEADS),
        in_specs=[pl.BlockSpec(memory_space=pltpu.SMEM)]
        + [slab(g * B_HEADS) for g in range(n_groups)]
        + [slab(n_groups * B_HEADS), slab((n_groups + 1) * B_HEADS)],
        out_specs=pl.BlockSpec((seq, HEAD_DIM), lambda b, h: (b, h)),
        out_shape=jax.ShapeDtypeStruct((batch * seq, B_HEADS * HEAD_DIM), BF16),
        scratch_shapes=[pltpu.VMEM((n_groups, seq, HEAD_DIM), F32),
                        pltpu.VMEM((n_groups, seq, LANES), F32),
                        pltpu.VMEM((n_groups, 3, B_TILE_Q, width_max), F32),
                        pltpu.VMEM((3, seq, HEAD_DIM), F32),
                        pltpu.VMEM((2, seq, HEAD_DIM), F32)],
        compiler_params=pltpu.CompilerParams(
            dimension_semantics=("parallel", "arbitrary"), vmem_limit_bytes=VMEM_LIMIT_BYTES),
        name="attn_b",
    )(c, qkv, qkv, qkv, qkv, qkv)


C_POS_BITS = 6
C_SLOPE_PARTS = 3
C_AUG_POS = 2 * C_SLOPE_PARTS
C_STRIP = 32


def _attn_c_consts():
    c = (_alibi_slopes(C_HEADS).astype(np.float64) * math.log2(math.e)).astype(np.float32)
    parts, rest = [], c.copy()
    for _ in range(C_SLOPE_PARTS):
        p = rest.astype(BF16).astype(np.float32)
        parts.append(p)
        rest = rest - p
    qc = np.zeros((C_HEADS, 1, LANES), np.float32)
    kc = np.zeros((C_HEADS, 1, LANES), np.float32)
    for a, p in enumerate(parts):
        qc[:, 0, C_AUG_POS + 2 * a] = p
        qc[:, 0, C_AUG_POS + 2 * a + 1] = p
        kc[:, 0, 2 * a] = p * (1 << C_POS_BITS)
        kc[:, 0, 2 * a + 1] = p
    return jnp.asarray(c), jnp.asarray(qc), jnp.asarray(kc)


def _attn_c_kernel(c_ref, qc_ref, kc_ref, q_ref, k_ref, v_ref, lam_ref, sg_ref, o_ref,
                   kaug_ref, qv_ref, bias_ref, u_ref, p_ref, m_ref, l_ref, alpha_ref, acc_ref,
                   *, tq, tk, seq, lambda_init):
    h = pl.program_id(1)
    qi = pl.program_id(2)
    nk = seq // tk
    ratio = tk // tq
    kd = qi // ratio
    off = qi - kd * ratio
    lane = lax.broadcasted_iota(jnp.int32, (1, LANES), 1)
    hi_lane = (lane & 1) == 0
    lo_mask = (1 << C_POS_BITS) - 1

    @pl.when(qi == 0)
    def _():
        key_pos_lane = (lane >= C_AUG_POS) & (lane < 2 * C_AUG_POS)

        def fill(kb, carry):
            rows = pl.ds(pl.multiple_of(kb * tk, tk), tk)
            pos = kb * tk + lax.broadcasted_iota(jnp.int32, (tk, 1), 0)
            hi = (pos - (pos & lo_mask)).astype(F32)
            lo = (pos & lo_mask).astype(F32)
            aug = jnp.where(key_pos_lane, -jnp.where(hi_lane, hi, lo), kc_ref[0]).astype(BF16)
            for j in range(2):
                kaug_ref[j, rows, 0:HEAD_DIM] = k_ref[rows, j * HEAD_DIM:(j + 1) * HEAD_DIM]
                kaug_ref[j, rows, HEAD_DIM:2 * HEAD_DIM] = aug
            return carry

        lax.fori_loop(0, nk, fill, 0)
        jj = lax.broadcasted_iota(jnp.int32, (1, tk), 1)
        for o in range(ratio):
            ii = o * tq + lax.broadcasted_iota(jnp.int32, (tq, 1), 0)
            bias_ref[o] = c_ref[h] * jnp.abs(ii - jj).astype(F32)

    qpos = qi * tq + lax.broadcasted_iota(jnp.int32, (tq, 1), 0)
    q_hi = (qpos >> C_POS_BITS).astype(F32)
    q_lo = (qpos & lo_mask).astype(F32)
    q_aug = jnp.where(lane < C_AUG_POS, jnp.where(hi_lane, q_hi, q_lo), qc_ref[0])
    for j in range(2):
        qj = q_ref[:, j * HEAD_DIM:(j + 1) * HEAD_DIM]
        for variant, aug in enumerate((-q_aug, q_aug, jnp.zeros_like(q_aug))):
            qv_ref[variant, j, :, 0:HEAD_DIM] = qj
            qv_ref[variant, j, :, HEAD_DIM:2 * HEAD_DIM] = aug.astype(BF16)

    def key_rows(step):
        kb = kd + step
        kb = jnp.where(kb >= nk, kb - nk, kb)
        return kb, pl.ds(pl.multiple_of(kb * tk, tk), tk)

    def scores(step, slot):
        kb, rows = key_rows(step)
        variant = 2 if step == 0 else jnp.where(kb < kd, 0, 1)
        for j in range(2):
            u_ref[slot, j] = _dot_nt(qv_ref[variant, j], kaug_ref[j, rows, :])

    def accumulate(step, slot):
        _, rows = key_rows(step)
        vblk = v_ref[rows, :]
        for j in range(2):
            for r0 in range(0, tq, C_STRIP):
                rs = slice(r0, r0 + C_STRIP)
                u = u_ref[slot, j, rs, :]
                if step == 0:
                    u = u - bias_ref[off, rs, :]
                    m_new = jnp.broadcast_to(jnp.max(u, axis=1, keepdims=True), (C_STRIP, LANES))
                else:
                    m_prev = m_ref[j, rs, :]
                    m_new = jnp.maximum(m_prev, jnp.max(u, axis=1, keepdims=True))
                    alpha = jnp.exp2(m_prev - m_new)
                    alpha_ref[j, rs, :] = alpha
                p = jnp.exp2(u - jnp.tile(m_new, (1, tk // LANES)))
                psum = p[:, 0:LANES]
                for cb in range(1, tk // LANES):
                    psum = psum + p[:, cb * LANES:(cb + 1) * LANES]
                l_ref[j, rs, :] = psum if step == 0 else alpha * l_ref[j, rs, :] + psum
                m_ref[j, rs, :] = m_new
                p_ref[slot, j, rs, :] = p.astype(BF16)
            pv = _dot(p_ref[slot, j], vblk)
            acc_ref[j] = pv if step == 0 else jnp.tile(alpha_ref[j], (1, 2)) * acc_ref[j] + pv

    scores(0, 0)
    for step in range(nk):
        if step + 1 < nk:
            scores(step + 1, (step + 1) % 2)
        accumulate(step, step % 2)

    lp = lam_ref[...]
    lam = (jnp.exp(jnp.sum(lp[0:1] * lp[1:2], axis=-1, keepdims=True))
           - jnp.exp(jnp.sum(lp[2:3] * lp[3:4], axis=-1, keepdims=True)) + lambda_init)
    den0 = jnp.sum(l_ref[0], axis=1, keepdims=True)
    den1 = jnp.sum(l_ref[1], axis=1, keepdims=True)
    o = acc_ref[0] / den0 - lam * (acc_ref[1] / den1)
    o = (_rms(o) * sg_ref[...]) * (1.0 - lambda_init)
    o_ref[...] = o.astype(o_ref.dtype)


def _attn_c(proj, lam_params, subln_g, wl, lambda_init, batch, seq, tq=512, tk=1024):
    wh = 2 * HEAD_DIM
    d = C_HEADS * wh
    tk = min(tk, seq)
    tq = min(tq, tk)
    nq = seq // tq
    assert seq <= (1 << (2 * C_POS_BITS)) and tk % tq == 0 and tq % C_STRIP == 0 and tk % LANES == 0
    c, qc, kc = _attn_c_consts()
    const_spec = pl.BlockSpec((1, 1, LANES), lambda b, h, qi: (h, 0, 0))
    return pl.pallas_call(
        functools.partial(_attn_c_kernel, tq=tq, tk=tk, seq=seq, lambda_init=lambda_init),
        grid=(batch, C_HEADS, nq),
        in_specs=[
            pl.BlockSpec(memory_space=pltpu.SMEM),
            const_spec, const_spec,
            pl.BlockSpec((tq, wh), lambda b, h, qi: (b * nq + qi, h)),
            pl.BlockSpec((seq, wh), lambda b, h, qi: (b, C_HEADS + h)),
            pl.BlockSpec((seq, wh), lambda b, h, qi: (b, 2 * C_HEADS + h)),
            pl.BlockSpec((None, 4, HEAD_DIM), lambda b, h, qi: (wl, 0, 0)),
            pl.BlockSpec((None, 1, wh), lambda b, h, qi: (wl, 0, 0)),
        ],
        out_specs=pl.BlockSpec((tq, wh), lambda b, h, qi: (b * nq + qi, h)),
        out_shape=jax.ShapeDtypeStruct((batch * seq, d), BF16),
        scratch_shapes=[
            pltpu.VMEM((2, seq, wh), BF16),
            pltpu.VMEM((3, 2, tq, wh), BF16),
            pltpu.VMEM((tk // tq, tq, tk), F32),
            pltpu.VMEM((2, 2, tq, tk), F32),
            pltpu.VMEM((2, 2, tq, tk), BF16),
            pltpu.VMEM((2, tq, LANES), F32),
            pltpu.VMEM((2, tq, LANES), F32),
            pltpu.VMEM((2, tq, LANES), F32),
            pltpu.VMEM((2, tq, wh), F32),
        ],
        compiler_params=pltpu.CompilerParams(
            dimension_semantics=("parallel", "parallel", "arbitrary"),
            vmem_limit_bytes=VMEM_LIMIT_BYTES),
        name="attn_c",
    )(c, qc, kc, proj, proj, proj, lam_params, subln_g.reshape(-1, 1, wh))


def kernel(x, c, ada_w, ada_b, norm_g, ab_w_in, ab_w_out, a_sink, c_w_in, c_w_out, c_lambda,
           c_subln_g, ffn_w_up, ffn_conv_w, ffn_conv_b, ffn_w_down):
    batch, seq, d = x.shape
    depth = ada_w.shape[0]
    dff = ffn_w_down.shape[1]
    dff_pad = -(-dff // FFN_TF) * FFN_TF

    ab_w_in_b, ab_w_out_b = ab_w_in.astype(BF16), ab_w_out.astype(BF16)
    c_w_in_b, c_w_out_b = c_w_in.astype(BF16), c_w_out.astype(BF16)
    w_gate_up = _prep_w_up(ffn_w_up, dff_pad, layers=1)
    w_down = _prep_w_down(ffn_w_down, dff_pad, layers=1)
    conv_w = jnp.pad(ffn_conv_w, ((0, 0), (0, 0), (0, dff_pad - dff)))
    conv_b = jnp.pad(ffn_conv_b, ((0, 0), (0, dff_pad - dff))).reshape(depth, 1, dff_pad)

    q_scale = np.float32(HEAD_DIM ** -0.5 * LOG2E)
    ab_scale = np.ones((ab_w_in.shape[2],), np.float32)
    wq, wkv_a = A_Q_HEADS * HEAD_DIM, 2 * A_KV_HEADS * HEAD_DIM
    ab_scale[:wq] = q_scale
    ab_scale[wq + wkv_a:wq + wkv_a + len(B_PAIRS) * B_HEADS * HEAD_DIM] = q_scale
    ab_col_scale = jnp.asarray(ab_scale)
    c_col_scale = jnp.asarray(np.concatenate([np.full((d,), q_scale, np.float32),
                                              np.ones((2 * d,), np.float32)]))

    mod = _ada(c, ada_w, ada_b)
    xs = x.reshape(batch * seq, d)
    for layer in range(depth):
        wl = layer // 2
        if layer % 2 == 0:
            proj, slabs = _norm_proj(xs, norm_g, mod, layer, ab_w_in_b, wl, ab_col_scale, seq,
                                     tm=AB_PROJ_TM, tn=AB_PROJ_TN, slab_from=0)
            mixed = (_attn_a(proj, a_sink, wl, batch, seq),
                     _attn_b(slabs, batch, seq, first=A_Q_HEADS + 2 * A_KV_HEADS))
            xs = _out_proj(mixed, ab_w_out_b, wl, xs, norm_g, mod, layer, seq)
        else:
            lambda_init = 0.8 - 0.6 * math.exp(-0.3 * layer)
            proj = _norm_proj(xs, norm_g, mod, layer, c_w_in_b, wl, c_col_scale, seq, tn=C_PROJ_TN)
            mixed = _attn_c(proj, c_lambda, c_subln_g, wl, lambda_init, batch, seq)
            xs = _out_proj((mixed,), c_w_out_b, wl, xs, norm_g, mod, layer, seq)
        raw_next = (ffn_w_up, ffn_w_down) if layer + 1 < depth else None
        xs, w_gate_up, w_down = _ffn(xs, norm_g, mod, layer, w_gate_up, conv_w, conv_b, w_down, seq,
                                     raw_next=raw_next)
    return xs.reshape(batch, seq, d)
```

```python
import functools
import math

import numpy as np
import jax
import jax.numpy as jnp
from jax import lax
from jax.experimental import pallas as pl
from jax.experimental.pallas import tpu as pltpu

F32 = jnp.float32
BF16 = jnp.bfloat16

HEAD_DIM = 128
A_Q_HEADS = 4
A_KV_HEADS = 2
A_GROUP = A_Q_HEADS // A_KV_HEADS
A_RADIUS = 128
B_PAIRS = ((128, 1), (512, 4), (2048, 16))
B_HEADS = 4
N_MIX_HEADS = A_Q_HEADS + len(B_PAIRS) * B_HEADS
C_HEADS = 8
CONV_WIDTH = 3
EPS = 1e-6
NEG_INF = -1e30

SUBLANES_F32 = 8
SUBLANES_BF16 = 16
LANES = 128
VMEM_LIMIT_BYTES = 56 * 1024 * 1024

AB_BLOCK_Q = 256
FFN_HALO = SUBLANES_BF16
FFN_TF = 512
AB_PROJ_TM = 512
AB_PROJ_TN = 1792
C_PROJ_TN = 1536
LOG2E = math.log2(math.e)


def _alibi_slopes(n):
    return (2.0 ** (-8.0 * np.arange(1, n + 1, dtype=np.float32) / n)).astype(np.float32)


def _rms(x):
    return x * lax.rsqrt(jnp.mean(x * x, axis=-1, keepdims=True) + EPS)


def _dot(a, b):
    return jnp.dot(a, b, preferred_element_type=F32)


def _dot_nt(a, b):
    return lax.dot_general(a, b, (((1,), (1,)), ((), ())), preferred_element_type=F32)


def _ada_rows(cond_ref, w_ref, b_ref, o_ref):
    batch, d, _ = cond_ref.shape
    tn = w_ref.shape[1]
    acc = [jnp.zeros((SUBLANES_F32, tn), F32) for _ in range(batch)]
    for r0 in range(0, d, SUBLANES_F32):
        w8 = w_ref[r0:r0 + SUBLANES_F32, :]
        for b in range(batch):
            acc[b] = acc[b] + w8 * jnp.tile(cond_ref[b, r0:r0 + SUBLANES_F32, :], (1, tn // LANES))
    for b in range(batch):
        o_ref[b:b + 1, :] = jnp.sum(acc[b], axis=0, keepdims=True) + b_ref[...]
    o_ref[batch:, :] = jnp.zeros((o_ref.shape[0] - batch, tn), F32)


def _ada_kernel(c_ref, w_ref, b_ref, o_ref, cond_ref):
    @pl.when(pl.program_id(0) == 0)
    def _():
        c = c_ref[...]
        cond_ref[...] = c * jax.nn.sigmoid(c)

    _ada_rows(cond_ref, w_ref, b_ref, o_ref)


def _ada(c_lanes, ada_w, ada_b3, layer, tn=1024):
    _, d, n = ada_w.shape
    assert c_lanes.shape[0] < SUBLANES_F32
    c_spec = pl.BlockSpec(c_lanes.shape, lambda j: (0, 0, 0))
    return pl.pallas_call(
        _ada_kernel,
        grid=(n // tn,),
        in_specs=[
            c_spec,
            pl.BlockSpec((None, d, tn), lambda j: (layer, 0, j)),
            pl.BlockSpec((None, 1, tn), lambda j: (layer, 0, j)),
        ],
        out_specs=[pl.BlockSpec((SUBLANES_F32, tn), lambda j: (0, j)), c_spec],
        out_shape=[jax.ShapeDtypeStruct((SUBLANES_F32, n), F32),
                   jax.ShapeDtypeStruct(c_lanes.shape, F32)],
        compiler_params=pltpu.CompilerParams(
            dimension_semantics=("arbitrary",), vmem_limit_bytes=VMEM_LIMIT_BYTES),
        name="ada",
    )(c_lanes, ada_w, ada_b3)


MOD_SH1, MOD_SC1, MOD_G1, MOD_SH2, MOD_SC2, MOD_G2 = range(6)
NORM_PRE_MIX, NORM_POST_MIX, NORM_PRE_FFN, NORM_POST_FFN = range(4)


def _mod_spec(chunk, d):
    return pl.BlockSpec((SUBLANES_F32, d), lambda *_: (0, chunk))


def _norm_spec(layer, d):
    return pl.BlockSpec((None, 4, d), lambda *_: (layer, 0, 0))


def _batch_row(ref, b):
    return ref[pl.ds(b, 1), :]


NORM_STRIP = 32
NORM_UNROLL = 8


def _strip_rows(s):
    return pl.ds(pl.multiple_of(s * NORM_STRIP, NORM_STRIP), NORM_STRIP)


def _row_rsqrt_ms(src_ref, rs_ref):
    n_rows, d = src_ref.shape

    def body(s, carry):
        rows = _strip_rows(s)
        x = src_ref[rows, :]
        ms = jnp.sum(x * x, axis=1, keepdims=True) * (1.0 / d)
        rs_ref[rows, :] = jnp.broadcast_to(lax.rsqrt(ms + EPS), (NORM_STRIP, LANES))
        return carry

    lax.fori_loop(0, n_rows // NORM_STRIP, body, 0, unroll=NORM_UNROLL)


def _scaled_rows(src_ref, rs_ref, rows):
    return src_ref[rows, :] * jnp.tile(rs_ref[rows, :], (1, src_ref.shape[1] // LANES))


def _cast_pad_kernel(w_ref, o_ref, *, axis):
    n = w_ref.shape[axis]
    if axis == 0:
        o_ref[:n, :] = w_ref[...].astype(o_ref.dtype)
        o_ref[n:, :] = jnp.zeros((o_ref.shape[0] - n, o_ref.shape[1]), o_ref.dtype)
    else:
        o_ref[:, :n] = w_ref[...].astype(o_ref.dtype)
        o_ref[:, n:] = jnp.zeros((o_ref.shape[0], o_ref.shape[1] - n), o_ref.dtype)


def _prep_w_up(w_up, dff_pad, layers, tr=256):
    _, d, two_dff = w_up.shape
    dff = two_dff // 2
    return pl.pallas_call(
        functools.partial(_cast_pad_kernel, axis=1),
        grid=(layers, 2, d // tr),
        in_specs=[pl.BlockSpec((None, tr, dff), lambda l, p, r: (l, r, p))],
        out_specs=pl.BlockSpec((None, None, tr, dff_pad), lambda l, p, r: (l, p, r, 0)),
        out_shape=jax.ShapeDtypeStruct((layers, 2, d, dff_pad), BF16),
        compiler_params=pltpu.CompilerParams(
            dimension_semantics=("arbitrary",) * 3, vmem_limit_bytes=VMEM_LIMIT_BYTES),
        name="prep_w_up",
    )(w_up)


def _prep_w_down(w_down, dff_pad, layers, tc=256):
    _, dff, d = w_down.shape
    return pl.pallas_call(
        functools.partial(_cast_pad_kernel, axis=0),
        grid=(layers, d // tc),
        in_specs=[pl.BlockSpec((None, dff, tc), lambda l, c: (l, 0, c))],
        out_specs=pl.BlockSpec((None, dff_pad, tc), lambda l, c: (l, 0, c)),
        out_shape=jax.ShapeDtypeStruct((layers, dff_pad, d), BF16),
        compiler_params=pltpu.CompilerParams(
            dimension_semantics=("arbitrary",) * 2, vmem_limit_bytes=VMEM_LIMIT_BYTES),
        name="prep_w_down",
    )(w_down)


def _norm_proj_kernel(x_ref, g_ref, sc_ref, sh_ref, w_ref, cs_ref, o_ref, *rest, per_seq, slab_from):
    if slab_from is None:
        h_ref, rs_ref = rest
    else:
        slab_ref, h_ref, rs_ref = rest
    j = pl.program_id(1)

    @pl.when(j == 0)
    def _():
        b = pl.program_id(0) // per_seq
        g_mod = g_ref[NORM_PRE_MIX:NORM_PRE_MIX + 1, :] * (1.0 + _batch_row(sc_ref, b))
        shift = _batch_row(sh_ref, b)
        _row_rsqrt_ms(x_ref, rs_ref)

        def body(s, carry):
            rows = _strip_rows(s)
            h_ref[rows, :] = (_scaled_rows(x_ref, rs_ref, rows) * g_mod + shift).astype(BF16)
            return carry

        lax.fori_loop(0, x_ref.shape[0] // NORM_STRIP, body, 0, unroll=NORM_UNROLL)

    res = _dot(h_ref[...], w_ref[...]) * cs_ref[...]
    o_ref[...] = res.astype(o_ref.dtype)
    if slab_from is not None:
        @pl.when(j >= slab_from)
        def _():
            for hh in range(slab_ref.shape[0]):
                slab_ref[hh] = res[:, hh * LANES:(hh + 1) * LANES]


def _norm_proj(x, norm_g, mod, layer, w, wl, col_scale, seq, tm=1024, tn=512, slab_from=None):
    m, d = x.shape
    n = w.shape[2]
    tm = min(tm, seq)
    nj = n // tn
    out_specs = pl.BlockSpec((tm, tn), lambda i, j: (i, j))
    out_shape = jax.ShapeDtypeStruct((m, n), BF16)
    if slab_from is not None:
        per_block = tn // LANES
        out_specs = [out_specs, pl.BlockSpec((per_block, tm, LANES),
                                             lambda i, j: (jnp.maximum(j - slab_from, 0), i, 0))]
        out_shape = [out_shape, jax.ShapeDtypeStruct(((nj - slab_from) * per_block, m, LANES), F32)]
    return pl.pallas_call(
        functools.partial(_norm_proj_kernel, per_seq=seq // tm, slab_from=slab_from),
        grid=(m // tm, nj),
        in_specs=[
            pl.BlockSpec((tm, d), lambda i, j: (i, 0)),
            _norm_spec(layer, d),
            _mod_spec(MOD_SC1, d),
            _mod_spec(MOD_SH1, d),
            pl.BlockSpec((None, d, tn), lambda i, j: (wl, 0, j)),
            pl.BlockSpec((1, tn), lambda i, j: (0, j)),
        ],
        out_specs=out_specs,
        out_shape=out_shape,
        scratch_shapes=[pltpu.VMEM((tm, d), BF16), pltpu.VMEM((tm, LANES), F32)],
        compiler_params=pltpu.CompilerParams(
            dimension_semantics=("parallel", "arbitrary"), vmem_limit_bytes=VMEM_LIMIT_BYTES),
        name="norm_proj",
    )(x, norm_g, mod, mod, w, col_scale.reshape(1, n))


def _out_proj_kernel(*refs, per_seq, widths, ada_next):
    a_refs = refs[:len(widths)]
    if ada_next:
        w_ref, x_ref, g_ref, gate_ref, c_ref, aw_ref, ab_ref, o_ref, mod_ref = refs[len(widths):]
        _ada_rows(c_ref, aw_ref, ab_ref, mod_ref)
    else:
        w_ref, x_ref, g_ref, gate_ref, o_ref = refs[len(widths):]
    b = pl.program_id(0) // per_seq
    y, k0 = None, 0
    for a_ref, k in zip(a_refs, widths):
        part = _dot(a_ref[...], w_ref[k0:k0 + k, :])
        y = part if y is None else y + part
        k0 += k
    g = g_ref[NORM_POST_MIX:NORM_POST_MIX + 1, :]
    o_ref[...] = x_ref[...] + _batch_row(gate_ref, b) * (_rms(y) * g)


def _out_proj(mixed, w, wl, x, norm_g, mod, layer, seq, ada_next=None, tm=512):
    m = x.shape[0]
    widths = tuple(a.shape[1] for a in mixed)
    k, d = w.shape[1], w.shape[2]
    assert sum(widths) == k
    tm = min(tm, seq)
    steps = m // tm
    out_specs = pl.BlockSpec((tm, d), lambda i: (i, 0))
    out_shape = jax.ShapeDtypeStruct((m, d), F32)
    ada_specs, ada_args = [], []
    if ada_next is not None:
        c_lanes, ada_w, ada_b3 = ada_next
        n = ada_w.shape[2]
        tn = n // steps
        assert tn * steps == n and tn % LANES == 0
        ada_specs = [
            pl.BlockSpec(c_lanes.shape, lambda i: (0, 0, 0), pipeline_mode=pl.Buffered(1)),
            pl.BlockSpec((None, d, tn), lambda i: (layer + 1, 0, i)),
            pl.BlockSpec((None, 1, tn), lambda i: (layer + 1, 0, i)),
        ]
        ada_args = [c_lanes, ada_w, ada_b3]
        out_specs = [out_specs, pl.BlockSpec((SUBLANES_F32, tn), lambda i: (0, i))]
        out_shape = [out_shape, jax.ShapeDtypeStruct((SUBLANES_F32, n), F32)]
    return pl.pallas_call(
        functools.partial(_out_proj_kernel, per_seq=seq // tm, widths=widths,
                          ada_next=ada_next is not None),
        grid=(steps,),
        in_specs=[pl.BlockSpec((tm, kk), lambda i: (i, 0)) for kk in widths] + [
            pl.BlockSpec((None, k, d), lambda i: (wl, 0, 0)),
            pl.BlockSpec((tm, d), lambda i: (i, 0)),
            _norm_spec(layer, d),
            _mod_spec(MOD_G1, d),
        ] + ada_specs,
        out_specs=out_specs,
        out_shape=out_shape,
        compiler_params=pltpu.CompilerParams(
            dimension_semantics=("arbitrary",), vmem_limit_bytes=VMEM_LIMIT_BYTES),
        name="out_proj",
    )(*mixed, w, x, norm_g, mod, *ada_args)


def _ffn_kernel(x_ref, xp_ref, xn_ref, g_ref, sc_ref, sh_ref, wg_ref, wu_ref, cw_ref, cb_ref,
                wd_ref, gate_ref, *rest, tm, per_seq, prep_next, down_chunks):
    i = pl.program_id(0)
    f = pl.program_id(1)
    b = i // per_seq
    halo = FFN_HALO
    d = x_ref.shape[1]
    if prep_next:
        ng_ref, nu_ref, nd_ref, o_ref, ogu_ref, owd_ref, h_ref, rs_ref = rest
    else:
        o_ref, h_ref, rs_ref = rest

    @pl.when(f == 0)
    def _():
        g_mod = g_ref[NORM_PRE_FFN:NORM_PRE_FFN + 1, :] * (1.0 + _batch_row(sc_ref, b))
        shift = _batch_row(sh_ref, b)

        keep_prev = jnp.where(i % per_seq == 0, 0.0, 1.0)
        keep_next = jnp.where(i % per_seq == per_seq - 1, 0.0, 1.0)
        zeros = jnp.zeros((halo - SUBLANES_F32, d), F32)
        hp = (_rms(xp_ref[...]) * g_mod + shift) * keep_prev
        hn = (_rms(xn_ref[...]) * g_mod + shift) * keep_next
        h_ref[0:halo, :] = jnp.concatenate([zeros, hp], axis=0).astype(BF16)
        h_ref[halo + tm:, :] = jnp.concatenate([hn, zeros], axis=0).astype(BF16)

        _row_rsqrt_ms(x_ref, rs_ref)

        def body(s, carry):
            rows = _strip_rows(s)
            dst = pl.ds(pl.multiple_of(halo + s * NORM_STRIP, SUBLANES_BF16), NORM_STRIP)
            h_ref[dst, :] = (_scaled_rows(x_ref, rs_ref, rows) * g_mod + shift).astype(BF16)
            return carry

        lax.fori_loop(0, tm // NORM_STRIP, body, 0, unroll=NORM_UNROLL)
        o_ref[...] = jnp.zeros_like(o_ref)

    rows = tm + 2 * halo
    gate = _dot(h_ref[...], wg_ref[...])
    up = _dot(h_ref[halo:halo + tm, :], wu_ref[...])
    g_prev = pltpu.roll(gate, 1, 0)[halo:halo + tm]
    g_next = pltpu.roll(gate, rows - 1, 0)[halo:halo + tm]
    g_mid = gate[halo:halo + tm]
    cw = cw_ref[...]
    z = g_prev * cw[0:1] + g_mid * cw[1:2] + g_next * cw[2:3] + cb_ref[...]
    act = jax.nn.gelu(z, approximate=True) * up
    o_ref[...] += _dot(act.astype(BF16), wd_ref[...])

    if prep_next:
        dff = ng_ref.shape[1]
        for half, src in enumerate((ng_ref, nu_ref)):
            ogu_ref[half, :, 0:dff] = src[...].astype(BF16)
            ogu_ref[half, :, dff:] = jnp.zeros((src.shape[0], ogu_ref.shape[2] - dff), BF16)
        is_data = i * pl.num_programs(1) + f < down_chunks
        owd_ref[...] = jnp.where(is_data, nd_ref[...], 0.0).astype(BF16)

    @pl.when(f == pl.num_programs(1) - 1)
    def _():
        g_gate = g_ref[NORM_POST_FFN:NORM_POST_FFN + 1, :] * _batch_row(gate_ref, b)
        _row_rsqrt_ms(o_ref, rs_ref)

        def body(s, carry):
            rows = _strip_rows(s)
            o_ref[rows, :] = x_ref[rows, :] + _scaled_rows(o_ref, rs_ref, rows) * g_gate
            return carry

        lax.fori_loop(0, tm // NORM_STRIP, body, 0, unroll=NORM_UNROLL)


FFN_PREP_UP_ROWS = 32
FFN_PREP_DOWN_ROWS = 128


def _ffn(x, norm_g, mod, layer, w_gate_up, conv_w, conv_b, w_down, seq, raw_next=None,
         tm=1024, tf=FFN_TF):
    m, d = x.shape
    dff_pad = w_down.shape[1]
    tm = min(tm, seq)
    per_seq = seq // tm
    nblk8 = m // SUBLANES_F32
    r8 = tm // SUBLANES_F32
    nf = dff_pad // tf
    prep_specs, prep_args, prep_out_specs, prep_out_shapes, down_chunks = [], [], [], [], None
    if raw_next is not None:
        raw_up, raw_down = raw_next
        dff = raw_down.shape[1]
        steps = (m // tm) * nf
        up_rows = FFN_PREP_UP_ROWS
        while d // up_rows > steps:
            up_rows *= 2
        up_chunks = d // up_rows
        down_chunks = dff // FFN_PREP_DOWN_ROWS
        pad_chunks = (dff_pad - dff) // FFN_PREP_DOWN_ROWS
        assert d % up_rows == 0 and dff % FFN_PREP_DOWN_ROWS == 0
        assert pad_chunks * FFN_PREP_DOWN_ROWS == dff_pad - dff and pad_chunks <= 1
        assert down_chunks + pad_chunks <= steps
        nxt = layer + 1

        def up_chunk(i, f):
            return jnp.minimum(i * nf + f, up_chunks - 1)

        prep_specs = [
            pl.BlockSpec((None, up_rows, dff), lambda i, f: (nxt, up_chunk(i, f), 0)),
            pl.BlockSpec((None, up_rows, dff), lambda i, f: (nxt, up_chunk(i, f), 1)),
            pl.BlockSpec((None, FFN_PREP_DOWN_ROWS, d),
                         lambda i, f: (nxt, jnp.minimum(i * nf + f, down_chunks - 1), 0)),
        ]
        prep_args = [raw_up, raw_up, raw_down]
        prep_out_specs = [
            pl.BlockSpec((2, up_rows, dff_pad), lambda i, f: (0, up_chunk(i, f), 0)),
            pl.BlockSpec((FFN_PREP_DOWN_ROWS, d),
                         lambda i, f: (jnp.minimum(i * nf + f, down_chunks + pad_chunks - 1), 0)),
        ]
        prep_out_shapes = [jax.ShapeDtypeStruct((2, d, dff_pad), BF16),
                           jax.ShapeDtypeStruct((dff_pad, d), BF16)]
    x_out_spec = pl.BlockSpec((tm, d), lambda i, f: (i, 0), pipeline_mode=pl.Buffered(1))
    x_out_shape = jax.ShapeDtypeStruct((m, d), F32)
    outs = pl.pallas_call(
        functools.partial(_ffn_kernel, tm=tm, per_seq=per_seq, prep_next=raw_next is not None,
                          down_chunks=down_chunks),
        grid=(m // tm, nf),
        in_specs=[
            pl.BlockSpec((tm, d), lambda i, f: (i, 0)),
            pl.BlockSpec((SUBLANES_F32, d), lambda i, f: (jnp.maximum(i * r8 - 1, 0), 0)),
            pl.BlockSpec((SUBLANES_F32, d), lambda i, f: (jnp.minimum((i + 1) * r8, nblk8 - 1), 0)),
            _norm_spec(layer, d),
            _mod_spec(MOD_SC2, d),
            _mod_spec(MOD_SH2, d),
            pl.BlockSpec((None, None, d, tf), lambda i, f: (0, 0, 0, f)),
            pl.BlockSpec((None, None, d, tf), lambda i, f: (0, 1, 0, f)),
            pl.BlockSpec((None, CONV_WIDTH, tf), lambda i, f: (layer, 0, f)),
            pl.BlockSpec((None, 1, tf), lambda i, f: (layer, 0, f)),
            pl.BlockSpec((None, tf, d), lambda i, f: (0, f, 0)),
            _mod_spec(MOD_G2, d),
        ] + prep_specs,
        out_specs=[x_out_spec] + prep_out_specs,
        out_shape=[x_out_shape] + prep_out_shapes,
        scratch_shapes=[pltpu.VMEM((tm + 2 * FFN_HALO, d), BF16), pltpu.VMEM((tm, LANES), F32)],
        compiler_params=pltpu.CompilerParams(
            dimension_semantics=("arbitrary", "arbitrary"), vmem_limit_bytes=VMEM_LIMIT_BYTES),
        name="ffn",
    )(x, x, x, norm_g, mod, mod, w_gate_up, w_gate_up, conv_w, conv_b, w_down, mod, *prep_args)
    if raw_next is None:
        return outs[0], None, None
    return outs[0], outs[1][None], outs[2][None]


def _attn_a_kernel(c_ref, sink_ref, qa_ref, ka_ref, va_ref, o_ref, *, seq, wl):
    tq = AB_BLOCK_Q
    t0 = pl.program_id(1) * tq
    width = min(tq + 2 * A_RADIUS, seq)
    start = pl.multiple_of(jnp.clip(t0 - A_RADIUS, 0, seq - width), SUBLANES_BF16)
    qpos = t0 + lax.broadcasted_iota(jnp.int32, (tq, 1), 0)
    kpos = start + lax.broadcasted_iota(jnp.int32, (1, width), 1)
    dist = jnp.abs(qpos - kpos)
    valid = dist <= A_RADIUS
    distf = dist.astype(F32)
    for kvh in range(A_KV_HEADS):
        cols = slice(kvh * HEAD_DIM, (kvh + 1) * HEAD_DIM)
        kw = ka_ref[pl.ds(start, width), cols]
        vw = va_ref[pl.ds(start, width), cols]
        for gq in range(A_GROUP):
            hq = kvh * A_GROUP + gq
            u = _dot_nt(qa_ref[:, hq * HEAD_DIM:(hq + 1) * HEAD_DIM], kw)
            u = jnp.where(valid, u - c_ref[hq] * distf, NEG_INF)
            sink = sink_ref[wl, hq] * LOG2E
            m = jnp.maximum(u.max(axis=-1, keepdims=True), sink)
            p = jnp.exp2(u - m)
            den = p.sum(axis=-1, keepdims=True) + jnp.exp2(sink - m)
            o = _dot(p.astype(BF16), vw) / den
            o_ref[:, hq * HEAD_DIM:(hq + 1) * HEAD_DIM] = o.astype(o_ref.dtype)


def _attn_a(proj, sink, wl, batch, seq):
    tq = min(AB_BLOCK_Q, seq)
    nq = seq // tq
    wq = A_Q_HEADS * HEAD_DIM
    wka = A_KV_HEADS * HEAD_DIM
    assert wq % wka == 0
    c = jnp.asarray(_alibi_slopes(N_MIX_HEADS)[:A_Q_HEADS] * np.float32(LOG2E))
    smem = pl.BlockSpec(memory_space=pltpu.SMEM)
    return pl.pallas_call(
        functools.partial(_attn_a_kernel, seq=seq, wl=wl),
        grid=(batch, nq),
        in_specs=[
            smem, smem,
            pl.BlockSpec((tq, wq), lambda b, n: (b * nq + n, 0)),
            pl.BlockSpec((seq, wka), lambda b, n: (b, wq // wka)),
            pl.BlockSpec((seq, wka), lambda b, n: (b, wq // wka + 1)),
        ],
        out_specs=pl.BlockSpec((tq, wq), lambda b, n: (b * nq + n, 0)),
        out_shape=jax.ShapeDtypeStruct((batch * seq, wq), BF16),
        compiler_params=pltpu.CompilerParams(
            dimension_semantics=("parallel", "arbitrary"), vmem_limit_bytes=VMEM_LIMIT_BYTES),
        name="attn_a",
    )(c, sink, proj, proj, proj)


B_TILE_Q = 128
B_FOLD = 4
B_TILES_PER_STEP = 16


def _attn_b_kernel(c_ref, q0_ref, q1_ref, q2_ref, k_ref, v_ref, o_ref, og_ref, lse_ref, bias_ref,
                   fold_ref, ofold_ref, *, seq):
    hb = pl.program_id(1)
    tq = B_TILE_Q
    q_refs = (q0_ref, q1_ref, q2_ref)
    seg = seq // B_FOLD

    for g, (window_len, dil) in enumerate(B_PAIRS):
        radius = window_len // (2 * dil)
        width = min(tq + 2 * radius, seq // dil)
        slope = c_ref[g * B_HEADS + hb] * float(dil)
        for variant, offset in enumerate((0, radius, width - tq)):
            dist = jnp.abs((offset + lax.broadcasted_iota(jnp.int32, (tq, 1), 0))
                           - lax.broadcasted_iota(jnp.int32, (1, width), 1))
            bias_ref[g, variant, :, 0:width] = jnp.where(dist <= radius,
                                                         -slope * dist.astype(F32), NEG_INF)

    def fold_rows(r, first, n):
        return pl.ds((r % B_FOLD) * seg + r // B_FOLD + B_FOLD * first, n, stride=B_FOLD)

    def tile(g, dil, radius, r, blk):
        length = seq // dil
        width = min(tq + 2 * radius, length)
        i0 = blk * tq
        start = jnp.clip(i0 - radius, 0, length - width)
        if dil == 1:
            qrows = pl.ds(pl.multiple_of(i0, tq), tq)
            krows = pl.ds(pl.multiple_of(start, radius), width)
        elif dil == B_FOLD:
            qrows = pl.ds(r + dil * i0, tq, stride=dil)
            krows = pl.ds(r + dil * start, width, stride=dil)
        else:
            qrows = fold_rows(r, i0, tq)
            krows = fold_rows(r, start, width)
        if dil <= B_FOLD:
            q_src, k_src, v_src = q_refs[g], k_ref, v_ref
            og_dst, lse_dst = og_ref.at[g], lse_ref.at[g]
        else:
            q_src, k_src, v_src = fold_ref.at[0], fold_ref.at[1], fold_ref.at[2]
            og_dst, lse_dst = ofold_ref.at[0], ofold_ref.at[1]
        q = q_src[qrows, :].astype(BF16)
        k = k_src[krows, :].astype(BF16)
        v = v_src[krows, :].astype(BF16)
        variant = jnp.where(blk == 0, 0, jnp.where(blk == length // tq - 1, 2, 1))
        u = _dot_nt(q, k) + bias_ref[g, variant, :, 0:width]
        m = u.max(axis=-1, keepdims=True)
        p = jnp.exp2(u - m)
        den = p.sum(axis=-1, keepdims=True)
        og_dst[qrows, :] = _dot(p.astype(BF16), v) / den
        lse_dst[qrows, :] = jnp.broadcast_to(m + jnp.log2(den), (tq, LANES))

    for g, (window_len, dil) in enumerate(B_PAIRS):
        radius = window_len // (2 * dil)
        per_class = (seq // dil) // tq
        n_tiles = dil * per_class
        if dil > B_FOLD:
            for a, src in enumerate((q_refs[g], k_ref, v_ref)):
                for r in range(B_FOLD):
                    fold_ref[a, r * seg:(r + 1) * seg, :] = src[pl.ds(r, seg, stride=B_FOLD), :]

        def body(s, carry, g=g, dil=dil, radius=radius, per_class=per_class):
            for k in range(B_TILES_PER_STEP):
                t = s * B_TILES_PER_STEP + k
                tile(g, dil, radius, t // per_class, t % per_class)
            return carry

        lax.fori_loop(0, n_tiles // B_TILES_PER_STEP, body, 0)
        if dil > B_FOLD:
            for r in range(B_FOLD):
                og_ref[g, pl.ds(r, seg, stride=B_FOLD), :] = ofold_ref[0, r * seg:(r + 1) * seg, :]
                lse_ref[g, pl.ds(r, seg, stride=B_FOLD), :] = ofold_ref[1, r * seg:(r + 1) * seg, :]

    def merge(s, carry):
        rows = pl.ds(pl.multiple_of(s * tq, tq), tq)
        lse = [lse_ref[g, rows, :] for g in range(len(B_PAIRS))]
        top = functools.reduce(jnp.maximum, lse)
        wts = [jnp.exp2(x - top) for x in lse]
        num = sum(w * og_ref[g, rows, :] for g, w in enumerate(wts))
        o_ref[rows, :] = (num / sum(wts)).astype(o_ref.dtype)
        return carry

    lax.fori_loop(0, seq // tq, merge, 0, unroll=2)


def _attn_b(qkv, batch, seq, first=0):
    n_groups = len(B_PAIRS)
    assert all(seq % (d * B_TILE_Q) == 0 and (seq // d // B_TILE_Q * d) % B_TILES_PER_STEP == 0
               and d in (1, B_FOLD, B_FOLD * B_FOLD) for _, d in B_PAIRS)
    c = jnp.asarray(_alibi_slopes(N_MIX_HEADS)[A_Q_HEADS:] * np.float32(LOG2E))
    width_max = max(min(B_TILE_Q + 2 * (w // (2 * d)), seq // d) for w, d in B_PAIRS)

    def slab(head0):
        return pl.BlockSpec((None, seq, HEAD_DIM), lambda b, h: (first + head0 + h, b, 0))

    return pl.pallas_call(
        functools.partial(_attn_b_kernel, seq=seq),
        grid=(batch, B_HEADS),
        in_specs=[pl.BlockSpec(memory_space=pltpu.SMEM)]
        + [slab(g * B_HEADS) for g in range(n_groups)]
        + [slab(n_groups * B_HEADS), slab((n_groups + 1) * B_HEADS)],
        out_specs=pl.BlockSpec((seq, HEAD_DIM), lambda b, h: (b, h)),
        out_shape=jax.ShapeDtypeStruct((batch * seq, B_HEADS * HEAD_DIM), BF16),
        scratch_shapes=[pltpu.VMEM((n_groups, seq, HEAD_DIM), F32),
                        pltpu.VMEM((n_groups, seq, LANES), F32),
                        pltpu.VMEM((n_groups, 3, B_TILE_Q, width_max), F32),
                        pltpu.VMEM((3, seq, HEAD_DIM), F32),
                        pltpu.VMEM((2, seq, HEAD_DIM), F32)],
        compiler_params=pltpu.CompilerParams(
            dimension_semantics=("parallel", "arbitrary"), vmem_limit_bytes=VMEM_LIMIT_BYTES),
        name="attn_b",
    )(c, qkv, qkv, qkv, qkv, qkv)


C_POS_BITS = 6
C_SLOPE_PARTS = 3
C_AUG_POS = 2 * C_SLOPE_PARTS
C_STRIP = 32


def _attn_c_consts():
    c = (_alibi_slopes(C_HEADS).astype(np.float64) * math.log2(math.e)).astype(np.float32)
    parts, rest = [], c.copy()
    for _ in range(C_SLOPE_PARTS):
        p = rest.astype(BF16).astype(np.float32)
        parts.append(p)
        rest = rest - p
    qc = np.zeros((C_HEADS, 1, LANES), np.float32)
    kc = np.zeros((C_HEADS, 1, LANES), np.float32)
    for a, p in enumerate(parts):
        qc[:, 0, C_AUG_POS + 2 * a] = p
        qc[:, 0, C_AUG_POS + 2 * a + 1] = p
        kc[:, 0, 2 * a] = p * (1 << C_POS_BITS)
        kc[:, 0, 2 * a + 1] = p
    return jnp.asarray(c), jnp.asarray(qc), jnp.asarray(kc)


def _attn_c_kernel(c_ref, qc_ref, kc_ref, q_ref, k_ref, v_ref, lam_ref, sg_ref, o_ref,
                   kaug_ref, qv_ref, bias_ref, u_ref, p_ref, m_ref, l_ref, alpha_ref, acc_ref,
                   *, tq, tk, seq, lambda_init):
    h = pl.program_id(1)
    qi = pl.program_id(2)
    nk = seq // tk
    ratio = tk // tq
    kd = qi // ratio
    off = qi - kd * ratio
    lane = lax.broadcasted_iota(jnp.int32, (1, LANES), 1)
    hi_lane = (lane & 1) == 0
    lo_mask = (1 << C_POS_BITS) - 1

    @pl.when(qi == 0)
    def _():
        key_pos_lane = (lane >= C_AUG_POS) & (lane < 2 * C_AUG_POS)

        def fill(kb, carry):
            rows = pl.ds(pl.multiple_of(kb * tk, tk), tk)
            pos = kb * tk + lax.broadcasted_iota(jnp.int32, (tk, 1), 0)
            hi = (pos - (pos & lo_mask)).astype(F32)
            lo = (pos & lo_mask).astype(F32)
            aug = jnp.where(key_pos_lane, -jnp.where(hi_lane, hi, lo), kc_ref[0]).astype(BF16)
            for j in range(2):
                kaug_ref[j, rows, 0:HEAD_DIM] = k_ref[rows, j * HEAD_DIM:(j + 1) * HEAD_DIM]
                kaug_ref[j, rows, HEAD_DIM:2 * HEAD_DIM] = aug
            return carry

        lax.fori_loop(0, nk, fill, 0)
        jj = lax.broadcasted_iota(jnp.int32, (1, tk), 1)
        for o in range(ratio):
            ii = o * tq + lax.broadcasted_iota(jnp.int32, (tq, 1), 0)
            bias_ref[o] = c_ref[h] * jnp.abs(ii - jj).astype(F32)

    qpos = qi * tq + lax.broadcasted_iota(jnp.int32, (tq, 1), 0)
    q_hi = (qpos >> C_POS_BITS).astype(F32)
    q_lo = (qpos & lo_mask).astype(F32)
    q_aug = jnp.where(lane < C_AUG_POS, jnp.where(hi_lane, q_hi, q_lo), qc_ref[0])
    for j in range(2):
        qj = q_ref[:, j * HEAD_DIM:(j + 1) * HEAD_DIM]
        for variant, aug in enumerate((-q_aug, q_aug, jnp.zeros_like(q_aug))):
            qv_ref[variant, j, :, 0:HEAD_DIM] = qj
            qv_ref[variant, j, :, HEAD_DIM:2 * HEAD_DIM] = aug.astype(BF16)

    def key_rows(step):
        kb = kd + step
        kb = jnp.where(kb >= nk, kb - nk, kb)
        return kb, pl.ds(pl.multiple_of(kb * tk, tk), tk)

    def scores(step, slot):
        kb, rows = key_rows(step)
        variant = 2 if step == 0 else jnp.where(kb < kd, 0, 1)
        for j in range(2):
            u_ref[slot, j] = _dot_nt(qv_ref[variant, j], kaug_ref[j, rows, :])

    def accumulate(step, slot):
        _, rows = key_rows(step)
        vblk = v_ref[rows, :]
        for j in range(2):
            for r0 in range(0, tq, C_STRIP):
                rs = slice(r0, r0 + C_STRIP)
                u = u_ref[slot, j, rs, :]
                if step == 0:
                    u = u - bias_ref[off, rs, :]
                    m_new = jnp.broadcast_to(jnp.max(u, axis=1, keepdims=True), (C_STRIP, LANES))
                else:
                    m_prev = m_ref[j, rs, :]
                    m_new = jnp.maximum(m_prev, jnp.max(u, axis=1, keepdims=True))
                    alpha = jnp.exp2(m_prev - m_new)
                    alpha_ref[j, rs, :] = alpha
                p = jnp.exp2(u - jnp.tile(m_new, (1, tk // LANES)))
                psum = p[:, 0:LANES]
                for cb in range(1, tk // LANES):
                    psum = psum + p[:, cb * LANES:(cb + 1) * LANES]
                l_ref[j, rs, :] = psum if step == 0 else alpha * l_ref[j, rs, :] + psum
                m_ref[j, rs, :] = m_new
                p_ref[slot, j, rs, :] = p.astype(BF16)
            pv = _dot(p_ref[slot, j], vblk)
            acc_ref[j] = pv if step == 0 else jnp.tile(alpha_ref[j], (1, 2)) * acc_ref[j] + pv

    scores(0, 0)
    for step in range(nk):
        if step + 1 < nk:
            scores(step + 1, (step + 1) % 2)
        accumulate(step, step % 2)

    lp = lam_ref[...]
    lam = (jnp.exp(jnp.sum(lp[0:1] * lp[1:2], axis=-1, keepdims=True))
           - jnp.exp(jnp.sum(lp[2:3] * lp[3:4], axis=-1, keepdims=True)) + lambda_init)
    den0 = jnp.sum(l_ref[0], axis=1, keepdims=True)
    den1 = jnp.sum(l_ref[1], axis=1, keepdims=True)
    o = acc_ref[0] / den0 - lam * (acc_ref[1] / den1)
    o = (_rms(o) * sg_ref[...]) * (1.0 - lambda_init)
    o_ref[...] = o.astype(o_ref.dtype)


def _attn_c(proj, lam_params, subln_g, wl, lambda_init, batch, seq, tq=512, tk=1024):
    wh = 2 * HEAD_DIM
    d = C_HEADS * wh
    tk = min(tk, seq)
    tq = min(tq, tk)
    nq = seq // tq
    assert seq <= (1 << (2 * C_POS_BITS)) and tk % tq == 0 and tq % C_STRIP == 0 and tk % LANES == 0
    c, qc, kc = _attn_c_consts()
    const_spec = pl.BlockSpec((1, 1, LANES), lambda b, h, qi: (h, 0, 0))
    return pl.pallas_call(
        functools.partial(_attn_c_kernel, tq=tq, tk=tk, seq=seq, lambda_init=lambda_init),
        grid=(batch, C_HEADS, nq),
        in_specs=[
            pl.BlockSpec(memory_space=pltpu.SMEM),
            const_spec, const_spec,
            pl.BlockSpec((tq, wh), lambda b, h, qi: (b * nq + qi, h)),
            pl.BlockSpec((seq, wh), lambda b, h, qi: (b, C_HEADS + h)),
            pl.BlockSpec((seq, wh), lambda b, h, qi: (b, 2 * C_HEADS + h)),
            pl.BlockSpec((None, 4, HEAD_DIM), lambda b, h, qi: (wl, 0, 0)),
            pl.BlockSpec((None, 1, wh), lambda b, h, qi: (wl, 0, 0)),
        ],
        out_specs=pl.BlockSpec((tq, wh), lambda b, h, qi: (b * nq + qi, h)),
        out_shape=jax.ShapeDtypeStruct((batch * seq, d), BF16),
        scratch_shapes=[
            pltpu.VMEM((2, seq, wh), BF16),
            pltpu.VMEM((3, 2, tq, wh), BF16),
            pltpu.VMEM((tk // tq, tq, tk), F32),
            pltpu.VMEM((2, 2, tq, tk), F32),
            pltpu.VMEM((2, 2, tq, tk), BF16),
            pltpu.VMEM((2, tq, LANES), F32),
            pltpu.VMEM((2, tq, LANES), F32),
            pltpu.VMEM((2, tq, LANES), F32),
            pltpu.VMEM((2, tq, wh), F32),
        ],
        compiler_params=pltpu.CompilerParams(
            dimension_semantics=("parallel", "parallel", "arbitrary"),
            vmem_limit_bytes=VMEM_LIMIT_BYTES),
        name="attn_c",
    )(c, qc, kc, proj, proj, proj, lam_params, subln_g.reshape(-1, 1, wh))


def kernel(x, c, ada_w, ada_b, norm_g, ab_w_in, ab_w_out, a_sink, c_w_in, c_w_out, c_lambda,
           c_subln_g, ffn_w_up, ffn_conv_w, ffn_conv_b, ffn_w_down):
    batch, seq, d = x.shape
    depth = ada_w.shape[0]
    dff = ffn_w_down.shape[1]
    dff_pad = -(-dff // FFN_TF) * FFN_TF

    ab_w_in_b, ab_w_out_b = ab_w_in.astype(BF16), ab_w_out.astype(BF16)
    c_w_in_b, c_w_out_b = c_w_in.astype(BF16), c_w_out.astype(BF16)
    w_gate_up = _prep_w_up(ffn_w_up, dff_pad, layers=1)
    w_down = _prep_w_down(ffn_w_down, dff_pad, layers=1)
    conv_w = jnp.pad(ffn_conv_w, ((0, 0), (0, 0), (0, dff_pad - dff)))
    conv_b = jnp.pad(ffn_conv_b, ((0, 0), (0, dff_pad - dff))).reshape(depth, 1, dff_pad)

    q_scale = np.float32(HEAD_DIM ** -0.5 * LOG2E)
    ab_scale = np.ones((ab_w_in.shape[2],), np.float32)
    wq, wkv_a = A_Q_HEADS * HEAD_DIM, 2 * A_KV_HEADS * HEAD_DIM
    ab_scale[:wq] = q_scale
    ab_scale[wq + wkv_a:wq + wkv_a + len(B_PAIRS) * B_HEADS * HEAD_DIM] = q_scale
    ab_col_scale = jnp.asarray(ab_scale)
    c_col_scale = jnp.asarray(np.concatenate([np.full((d,), q_scale, np.float32),
                                              np.ones((2 * d,), np.float32)]))

    c_lanes = jnp.broadcast_to(c[:, :, None], (batch, d, LANES))
    ada_b3 = ada_b.reshape(depth, 1, ada_b.shape[1])
    mod, cond_lanes = _ada(c_lanes, ada_w, ada_b3, 0)
    xs = x.reshape(batch * seq, d)
    for layer in range(depth):
        wl = layer // 2
        last = layer + 1 == depth
        ada_next = None if last else (cond_lanes, ada_w, ada_b3)
        if layer % 2 == 0:
            proj, slabs = _norm_proj(xs, norm_g, mod, layer, ab_w_in_b, wl, ab_col_scale, seq,
                                     tm=AB_PROJ_TM, tn=AB_PROJ_TN, slab_from=0)
            mixed = (_attn_a(proj, a_sink, wl, batch, seq),
                     _attn_b(slabs, batch, seq, first=A_Q_HEADS + 2 * A_KV_HEADS))
            w_out_b = ab_w_out_b
        else:
            lambda_init = 0.8 - 0.6 * math.exp(-0.3 * layer)
            proj = _norm_proj(xs, norm_g, mod, layer, c_w_in_b, wl, c_col_scale, seq, tn=C_PROJ_TN)
            mixed = (_attn_c(proj, c_lambda, c_subln_g, wl, lambda_init, batch, seq),)
            w_out_b = c_w_out_b
        res = _out_proj(mixed, w_out_b, wl, xs, norm_g, mod, layer, seq, ada_next=ada_next)
        xs, mod_next = (res, None) if last else res
        raw_next = None if last else (ffn_w_up, ffn_w_down)
        xs, w_gate_up, w_down = _ffn(xs, norm_g, mod, layer, w_gate_up, conv_w, conv_b, w_down, seq,
                                     raw_next=raw_next)
        mod = mod_next
    return xs.reshape(batch, seq, d)
```

```python
import functools
import math

import numpy as np
import jax
import jax.numpy as jnp
from jax import lax
from jax.experimental import pallas as pl
from jax.experimental.pallas import tpu as pltpu

F32 = jnp.float32
BF16 = jnp.bfloat16

HEAD_DIM = 128
A_Q_HEADS = 4
A_KV_HEADS = 2
A_GROUP = A_Q_HEADS // A_KV_HEADS
A_RADIUS = 128
B_PAIRS = ((128, 1), (512, 4), (2048, 16))
B_HEADS = 4
N_MIX_HEADS = A_Q_HEADS + len(B_PAIRS) * B_HEADS
C_HEADS = 8
CONV_WIDTH = 3
EPS = 1e-6
NEG_INF = -1e30

SUBLANES_F32 = 8
SUBLANES_BF16 = 16
LANES = 128
VMEM_LIMIT_BYTES = 56 * 1024 * 1024

AB_BLOCK_Q = 256
FFN_HALO = SUBLANES_BF16
FFN_TF = 512
AB_PROJ_TM = 512
AB_PROJ_TN = 1792
C_PROJ_TN = 1536
LOG2E = math.log2(math.e)


def _alibi_slopes(n):
    return (2.0 ** (-8.0 * np.arange(1, n + 1, dtype=np.float32) / n)).astype(np.float32)


def _rms(x):
    return x * lax.rsqrt(jnp.mean(x * x, axis=-1, keepdims=True) + EPS)


def _dot(a, b):
    return jnp.dot(a, b, preferred_element_type=F32)


def _dot_nt(a, b):
    return lax.dot_general(a, b, (((1,), (1,)), ((), ())), preferred_element_type=F32)


def _ada_rows(cond_ref, w_ref, b_ref, o_ref):
    batch, d, _ = cond_ref.shape
    tn = w_ref.shape[1]
    acc = [jnp.zeros((SUBLANES_F32, tn), F32) for _ in range(batch)]
    for r0 in range(0, d, SUBLANES_F32):
        w8 = w_ref[r0:r0 + SUBLANES_F32, :]
        for b in range(batch):
            acc[b] = acc[b] + w8 * jnp.tile(cond_ref[b, r0:r0 + SUBLANES_F32, :], (1, tn // LANES))
    for b in range(batch):
        o_ref[b:b + 1, :] = jnp.sum(acc[b], axis=0, keepdims=True) + b_ref[...]
    o_ref[batch:, :] = jnp.zeros((o_ref.shape[0] - batch, tn), F32)


def _ada_kernel(c_ref, w_ref, b_ref, o_ref, cond_ref):
    @pl.when(pl.program_id(0) == 0)
    def _():
        c = c_ref[...]
        cond_ref[...] = c * jax.nn.sigmoid(c)

    _ada_rows(cond_ref, w_ref, b_ref, o_ref)


def _ada(c_lanes, ada_w, ada_b3, layer, tn=1024):
    _, d, n = ada_w.shape
    assert c_lanes.shape[0] < SUBLANES_F32
    c_spec = pl.BlockSpec(c_lanes.shape, lambda j: (0, 0, 0))
    return pl.pallas_call(
        _ada_kernel,
        grid=(n // tn,),
        in_specs=[
            c_spec,
            pl.BlockSpec((None, d, tn), lambda j: (layer, 0, j)),
            pl.BlockSpec((None, 1, tn), lambda j: (layer, 0, j)),
        ],
        out_specs=[pl.BlockSpec((SUBLANES_F32, tn), lambda j: (0, j)), c_spec],
        out_shape=[jax.ShapeDtypeStruct((SUBLANES_F32, n), F32),
                   jax.ShapeDtypeStruct(c_lanes.shape, F32)],
        compiler_params=pltpu.CompilerParams(
            dimension_semantics=("arbitrary",), vmem_limit_bytes=VMEM_LIMIT_BYTES),
        name="ada",
    )(c_lanes, ada_w, ada_b3)


MOD_SH1, MOD_SC1, MOD_G1, MOD_SH2, MOD_SC2, MOD_G2 = range(6)
NORM_PRE_MIX, NORM_POST_MIX, NORM_PRE_FFN, NORM_POST_FFN = range(4)


def _mod_spec(chunk, d):
    return pl.BlockSpec((SUBLANES_F32, d), lambda *_: (0, chunk))


def _norm_spec(layer, d):
    return pl.BlockSpec((None, 4, d), lambda *_: (layer, 0, 0))


def _batch_row(ref, b):
    return ref[pl.ds(b, 1), :]


NORM_STRIP = 32
NORM_UNROLL = 8


def _strip_rows(s):
    return pl.ds(pl.multiple_of(s * NORM_STRIP, NORM_STRIP), NORM_STRIP)


def _row_rsqrt_ms(src_ref, rs_ref):
    n_rows, d = src_ref.shape

    def body(s, carry):
        rows = _strip_rows(s)
        x = src_ref[rows, :]
        ms = jnp.sum(x * x, axis=1, keepdims=True) * (1.0 / d)
        rs_ref[rows, :] = jnp.broadcast_to(lax.rsqrt(ms + EPS), (NORM_STRIP, LANES))
        return carry

    lax.fori_loop(0, n_rows // NORM_STRIP, body, 0, unroll=NORM_UNROLL)


def _scaled_rows(src_ref, rs_ref, rows):
    return src_ref[rows, :] * jnp.tile(rs_ref[rows, :], (1, src_ref.shape[1] // LANES))


def _cast_pad_kernel(w_ref, o_ref, *, axis):
    n = w_ref.shape[axis]
    if axis == 0:
        o_ref[:n, :] = w_ref[...].astype(o_ref.dtype)
        o_ref[n:, :] = jnp.zeros((o_ref.shape[0] - n, o_ref.shape[1]), o_ref.dtype)
    else:
        o_ref[:, :n] = w_ref[...].astype(o_ref.dtype)
        o_ref[:, n:] = jnp.zeros((o_ref.shape[0], o_ref.shape[1] - n), o_ref.dtype)


def _prep_w_up(w_up, dff_pad, layers, tr=256):
    _, d, two_dff = w_up.shape
    dff = two_dff // 2
    return pl.pallas_call(
        functools.partial(_cast_pad_kernel, axis=1),
        grid=(layers, 2, d // tr),
        in_specs=[pl.BlockSpec((None, tr, dff), lambda l, p, r: (l, r, p))],
        out_specs=pl.BlockSpec((None, None, tr, dff_pad), lambda l, p, r: (l, p, r, 0)),
        out_shape=jax.ShapeDtypeStruct((layers, 2, d, dff_pad), BF16),
        compiler_params=pltpu.CompilerParams(
            dimension_semantics=("arbitrary",) * 3, vmem_limit_bytes=VMEM_LIMIT_BYTES),
        name="prep_w_up",
    )(w_up)


def _prep_w_down(w_down, dff_pad, layers, tc=256):
    _, dff, d = w_down.shape
    return pl.pallas_call(
        functools.partial(_cast_pad_kernel, axis=0),
        grid=(layers, d // tc),
        in_specs=[pl.BlockSpec((None, dff, tc), lambda l, c: (l, 0, c))],
        out_specs=pl.BlockSpec((None, dff_pad, tc), lambda l, c: (l, 0, c)),
        out_shape=jax.ShapeDtypeStruct((layers, dff_pad, d), BF16),
        compiler_params=pltpu.CompilerParams(
            dimension_semantics=("arbitrary",) * 2, vmem_limit_bytes=VMEM_LIMIT_BYTES),
        name="prep_w_down",
    )(w_down)


def _norm_proj_kernel(x_ref, g_ref, sc_ref, sh_ref, w_ref, cs_ref, o_ref, *rest, per_seq, slab_from):
    if slab_from is None:
        h_ref, rs_ref = rest
    else:
        slab_ref, h_ref, rs_ref = rest
    j = pl.program_id(1)

    @pl.when(j == 0)
    def _():
        b = pl.program_id(0) // per_seq
        g_mod = g_ref[NORM_PRE_MIX:NORM_PRE_MIX + 1, :] * (1.0 + _batch_row(sc_ref, b))
        shift = _batch_row(sh_ref, b)
        _row_rsqrt_ms(x_ref, rs_ref)

        def body(s, carry):
            rows = _strip_rows(s)
            h_ref[rows, :] = (_scaled_rows(x_ref, rs_ref, rows) * g_mod + shift).astype(BF16)
            return carry

        lax.fori_loop(0, x_ref.shape[0] // NORM_STRIP, body, 0, unroll=NORM_UNROLL)

    res = _dot(h_ref[...], w_ref[...]) * cs_ref[...]
    o_ref[...] = res.astype(o_ref.dtype)
    if slab_from is not None:
        @pl.when(j >= slab_from)
        def _():
            for hh in range(slab_ref.shape[0]):
                slab_ref[hh] = res[:, hh * LANES:(hh + 1) * LANES]


def _norm_proj(x, norm_g, mod, layer, w, wl, col_scale, seq, tm=1024, tn=512, slab_from=None):
    m, d = x.shape
    n = w.shape[2]
    tm = min(tm, seq)
    nj = n // tn
    out_specs = pl.BlockSpec((tm, tn), lambda i, j: (i, j))
    out_shape = jax.ShapeDtypeStruct((m, n), BF16)
    if slab_from is not None:
        per_block = tn // LANES
        out_specs = [out_specs, pl.BlockSpec((per_block, tm, LANES),
                                             lambda i, j: (jnp.maximum(j - slab_from, 0), i, 0))]
        out_shape = [out_shape, jax.ShapeDtypeStruct(((nj - slab_from) * per_block, m, LANES), F32)]
    return pl.pallas_call(
        functools.partial(_norm_proj_kernel, per_seq=seq // tm, slab_from=slab_from),
        grid=(m // tm, nj),
        in_specs=[
            pl.BlockSpec((tm, d), lambda i, j: (i, 0)),
            _norm_spec(layer, d),
            _mod_spec(MOD_SC1, d),
            _mod_spec(MOD_SH1, d),
            pl.BlockSpec((None, d, tn), lambda i, j: (wl, 0, j)),
            pl.BlockSpec((1, tn), lambda i, j: (0, j)),
        ],
        out_specs=out_specs,
        out_shape=out_shape,
        scratch_shapes=[pltpu.VMEM((tm, d), BF16), pltpu.VMEM((tm, LANES), F32)],
        compiler_params=pltpu.CompilerParams(
            dimension_semantics=("parallel", "arbitrary"), vmem_limit_bytes=VMEM_LIMIT_BYTES),
        name="norm_proj",
    )(x, norm_g, mod, mod, w, col_scale.reshape(1, n))


def _out_proj_kernel(*refs, per_seq, widths, ada_next):
    a_refs = refs[:len(widths)]
    if ada_next:
        w_ref, x_ref, g_ref, gate_ref, c_ref, aw_ref, ab_ref, o_ref, mod_ref = refs[len(widths):]
        _ada_rows(c_ref, aw_ref, ab_ref, mod_ref)
    else:
        w_ref, x_ref, g_ref, gate_ref, o_ref = refs[len(widths):]
    b = pl.program_id(0) // per_seq
    y, k0 = None, 0
    for a_ref, k in zip(a_refs, widths):
        part = _dot(a_ref[...], w_ref[k0:k0 + k, :])
        y = part if y is None else y + part
        k0 += k
    g = g_ref[NORM_POST_MIX:NORM_POST_MIX + 1, :]
    o_ref[...] = x_ref[...] + _batch_row(gate_ref, b) * (_rms(y) * g)


def _out_proj(mixed, w, wl, x, norm_g, mod, layer, seq, ada_next=None, tm=512):
    m = x.shape[0]
    widths = tuple(a.shape[1] for a in mixed)
    k, d = w.shape[1], w.shape[2]
    assert sum(widths) == k
    tm = min(tm, seq)
    steps = m // tm
    out_specs = pl.BlockSpec((tm, d), lambda i: (i, 0))
    out_shape = jax.ShapeDtypeStruct((m, d), F32)
    ada_specs, ada_args = [], []
    if ada_next is not None:
        c_lanes, ada_w, ada_b3 = ada_next
        n = ada_w.shape[2]
        tn = n // steps
        assert tn * steps == n and tn % LANES == 0
        ada_specs = [
            pl.BlockSpec(c_lanes.shape, lambda i: (0, 0, 0), pipeline_mode=pl.Buffered(1)),
            pl.BlockSpec((None, d, tn), lambda i: (layer + 1, 0, i)),
            pl.BlockSpec((None, 1, tn), lambda i: (layer + 1, 0, i)),
        ]
        ada_args = [c_lanes, ada_w, ada_b3]
        out_specs = [out_specs, pl.BlockSpec((SUBLANES_F32, tn), lambda i: (0, i))]
        out_shape = [out_shape, jax.ShapeDtypeStruct((SUBLANES_F32, n), F32)]
    return pl.pallas_call(
        functools.partial(_out_proj_kernel, per_seq=seq // tm, widths=widths,
                          ada_next=ada_next is not None),
        grid=(steps,),
        in_specs=[pl.BlockSpec((tm, kk), lambda i: (i, 0)) for kk in widths] + [
            pl.BlockSpec((None, k, d), lambda i: (wl, 0, 0)),
            pl.BlockSpec((tm, d), lambda i: (i, 0)),
            _norm_spec(layer, d),
            _mod_spec(MOD_G1, d),
        ] + ada_specs,
        out_specs=out_specs,
        out_shape=out_shape,
        compiler_params=pltpu.CompilerParams(
            dimension_semantics=("arbitrary",), vmem_limit_bytes=VMEM_LIMIT_BYTES),
        name="out_proj",
    )(*mixed, w, x, norm_g, mod, *ada_args)


def _ffn_kernel(x_ref, xp_ref, xn_ref, g_ref, sc_ref, sh_ref, wg_ref, wu_ref, cw_ref, cb_ref,
                wd_ref, gate_ref, *rest, tm, per_seq, prep_next, down_chunks):
    i = pl.program_id(0)
    f = pl.program_id(1)
    b = i // per_seq
    halo = FFN_HALO
    d = x_ref.shape[1]
    if prep_next:
        (ng_ref, nu_ref, nd_ref, mi_ref, mo_ref, o_ref, ogu_ref, owd_ref, omi_ref, omo_ref,
         h_ref, rs_ref) = rest
    else:
        o_ref, h_ref, rs_ref = rest

    @pl.when(f == 0)
    def _():
        g_mod = g_ref[NORM_PRE_FFN:NORM_PRE_FFN + 1, :] * (1.0 + _batch_row(sc_ref, b))
        shift = _batch_row(sh_ref, b)

        keep_prev = jnp.where(i % per_seq == 0, 0.0, 1.0)
        keep_next = jnp.where(i % per_seq == per_seq - 1, 0.0, 1.0)
        zeros = jnp.zeros((halo - SUBLANES_F32, d), F32)
        hp = (_rms(xp_ref[...]) * g_mod + shift) * keep_prev
        hn = (_rms(xn_ref[...]) * g_mod + shift) * keep_next
        h_ref[0:halo, :] = jnp.concatenate([zeros, hp], axis=0).astype(BF16)
        h_ref[halo + tm:, :] = jnp.concatenate([hn, zeros], axis=0).astype(BF16)

        _row_rsqrt_ms(x_ref, rs_ref)

        def body(s, carry):
            rows = _strip_rows(s)
            dst = pl.ds(pl.multiple_of(halo + s * NORM_STRIP, SUBLANES_BF16), NORM_STRIP)
            h_ref[dst, :] = (_scaled_rows(x_ref, rs_ref, rows) * g_mod + shift).astype(BF16)
            return carry

        lax.fori_loop(0, tm // NORM_STRIP, body, 0, unroll=NORM_UNROLL)
        o_ref[...] = jnp.zeros_like(o_ref)

    rows = tm + 2 * halo
    gate = _dot(h_ref[...], wg_ref[...])
    up = _dot(h_ref[halo:halo + tm, :], wu_ref[...])
    g_prev = pltpu.roll(gate, 1, 0)[halo:halo + tm]
    g_next = pltpu.roll(gate, rows - 1, 0)[halo:halo + tm]
    g_mid = gate[halo:halo + tm]
    cw = cw_ref[...]
    z = g_prev * cw[0:1] + g_mid * cw[1:2] + g_next * cw[2:3] + cb_ref[...]
    act = jax.nn.gelu(z, approximate=True) * up
    o_ref[...] += _dot(act.astype(BF16), wd_ref[...])

    if prep_next:
        dff = ng_ref.shape[1]
        for half, src in enumerate((ng_ref, nu_ref)):
            ogu_ref[half, :, 0:dff] = src[...].astype(BF16)
            ogu_ref[half, :, dff:] = jnp.zeros((src.shape[0], ogu_ref.shape[2] - dff), BF16)
        is_data = i * pl.num_programs(1) + f < down_chunks
        owd_ref[...] = jnp.where(is_data, nd_ref[...], 0.0).astype(BF16)
        omi_ref[...] = mi_ref[...].astype(BF16)
        omo_ref[...] = mo_ref[...].astype(BF16)

    @pl.when(f == pl.num_programs(1) - 1)
    def _():
        g_gate = g_ref[NORM_POST_FFN:NORM_POST_FFN + 1, :] * _batch_row(gate_ref, b)
        _row_rsqrt_ms(o_ref, rs_ref)

        def body(s, carry):
            rows = _strip_rows(s)
            o_ref[rows, :] = x_ref[rows, :] + _scaled_rows(o_ref, rs_ref, rows) * g_gate
            return carry

        lax.fori_loop(0, tm // NORM_STRIP, body, 0, unroll=NORM_UNROLL)


FFN_PREP_UP_ROWS = 32
FFN_PREP_DOWN_ROWS = 128


def _ffn(x, norm_g, mod, layer, w_gate_up, conv_w, conv_b, w_down, seq, raw_next=None,
         tm=1024, tf=FFN_TF):
    m, d = x.shape
    dff_pad = w_down.shape[1]
    tm = min(tm, seq)
    per_seq = seq // tm
    nblk8 = m // SUBLANES_F32
    r8 = tm // SUBLANES_F32
    nf = dff_pad // tf
    prep_specs, prep_args, prep_out_specs, prep_out_shapes, down_chunks = [], [], [], [], None
    if raw_next is not None:
        raw_up, raw_down, mix_in, mix_out, mix_layer = raw_next
        dff = raw_down.shape[1]
        steps = (m // tm) * nf
        up_rows = FFN_PREP_UP_ROWS
        while d // up_rows > steps:
            up_rows *= 2
        up_chunks = d // up_rows
        down_chunks = dff // FFN_PREP_DOWN_ROWS
        pad_chunks = (dff_pad - dff) // FFN_PREP_DOWN_ROWS
        assert d % up_rows == 0 and dff % FFN_PREP_DOWN_ROWS == 0
        assert pad_chunks * FFN_PREP_DOWN_ROWS == dff_pad - dff and pad_chunks <= 1
        assert down_chunks + pad_chunks <= steps
        nxt = layer + 1

        def up_chunk(i, f):
            return jnp.minimum(i * nf + f, up_chunks - 1)

        prep_specs = [
            pl.BlockSpec((None, up_rows, dff), lambda i, f: (nxt, up_chunk(i, f), 0)),
            pl.BlockSpec((None, up_rows, dff), lambda i, f: (nxt, up_chunk(i, f), 1)),
            pl.BlockSpec((None, FFN_PREP_DOWN_ROWS, d),
                         lambda i, f: (nxt, jnp.minimum(i * nf + f, down_chunks - 1), 0)),
        ]
        prep_args = [raw_up, raw_up, raw_down, mix_in, mix_out]
        for w in (mix_in, mix_out):
            rows = SUBLANES_BF16
            while w.shape[1] // rows > steps:
                rows *= 2
            assert w.shape[1] % rows == 0
            chunks = w.shape[1] // rows
            prep_specs.append(pl.BlockSpec(
                (None, rows, w.shape[2]),
                lambda i, f, chunks=chunks: (mix_layer, jnp.minimum(i * nf + f, chunks - 1), 0)))
        prep_out_specs = [
            pl.BlockSpec((2, up_rows, dff_pad), lambda i, f: (0, up_chunk(i, f), 0)),
            pl.BlockSpec((FFN_PREP_DOWN_ROWS, d),
                         lambda i, f: (jnp.minimum(i * nf + f, down_chunks + pad_chunks - 1), 0)),
        ]
        prep_out_shapes = [jax.ShapeDtypeStruct((2, d, dff_pad), BF16),
                           jax.ShapeDtypeStruct((dff_pad, d), BF16)]
        for spec, w in zip(prep_specs[3:], (mix_in, mix_out)):
            rows, chunks = spec.block_shape[1], w.shape[1] // spec.block_shape[1]
            prep_out_specs.append(pl.BlockSpec(
                (rows, w.shape[2]),
                lambda i, f, chunks=chunks: (jnp.minimum(i * nf + f, chunks - 1), 0)))
            prep_out_shapes.append(jax.ShapeDtypeStruct(w.shape[1:], BF16))
    x_out_spec = pl.BlockSpec((tm, d), lambda i, f: (i, 0), pipeline_mode=pl.Buffered(1))
    x_out_shape = jax.ShapeDtypeStruct((m, d), F32)
    outs = pl.pallas_call(
        functools.partial(_ffn_kernel, tm=tm, per_seq=per_seq, prep_next=raw_next is not None,
                          down_chunks=down_chunks),
        grid=(m // tm, nf),
        in_specs=[
            pl.BlockSpec((tm, d), lambda i, f: (i, 0)),
            pl.BlockSpec((SUBLANES_F32, d), lambda i, f: (jnp.maximum(i * r8 - 1, 0), 0)),
            pl.BlockSpec((SUBLANES_F32, d), lambda i, f: (jnp.minimum((i + 1) * r8, nblk8 - 1), 0)),
            _norm_spec(layer, d),
            _mod_spec(MOD_SC2, d),
            _mod_spec(MOD_SH2, d),
            pl.BlockSpec((None, None, d, tf), lambda i, f: (0, 0, 0, f)),
            pl.BlockSpec((None, None, d, tf), lambda i, f: (0, 1, 0, f)),
            pl.BlockSpec((None, CONV_WIDTH, tf), lambda i, f: (layer, 0, f)),
            pl.BlockSpec((None, 1, tf), lambda i, f: (layer, 0, f)),
            pl.BlockSpec((None, tf, d), lambda i, f: (0, f, 0)),
            _mod_spec(MOD_G2, d),
        ] + prep_specs,
        out_specs=[x_out_spec] + prep_out_specs,
        out_shape=[x_out_shape] + prep_out_shapes,
        scratch_shapes=[pltpu.VMEM((tm + 2 * FFN_HALO, d), BF16), pltpu.VMEM((tm, LANES), F32)],
        compiler_params=pltpu.CompilerParams(
            dimension_semantics=("arbitrary", "arbitrary"), vmem_limit_bytes=VMEM_LIMIT_BYTES),
        name="ffn",
    )(x, x, x, norm_g, mod, mod, w_gate_up, w_gate_up, conv_w, conv_b, w_down, mod, *prep_args)
    if raw_next is None:
        return outs[0], None, None, None, None
    return (outs[0],) + tuple(o[None] for o in outs[1:])


def _attn_a_kernel(c_ref, sink_ref, qa_ref, ka_ref, va_ref, o_ref, *, seq, wl):
    tq = AB_BLOCK_Q
    t0 = pl.program_id(1) * tq
    width = min(tq + 2 * A_RADIUS, seq)
    start = pl.multiple_of(jnp.clip(t0 - A_RADIUS, 0, seq - width), SUBLANES_BF16)
    qpos = t0 + lax.broadcasted_iota(jnp.int32, (tq, 1), 0)
    kpos = start + lax.broadcasted_iota(jnp.int32, (1, width), 1)
    dist = jnp.abs(qpos - kpos)
    valid = dist <= A_RADIUS
    distf = dist.astype(F32)
    for kvh in range(A_KV_HEADS):
        cols = slice(kvh * HEAD_DIM, (kvh + 1) * HEAD_DIM)
        kw = ka_ref[pl.ds(start, width), cols]
        vw = va_ref[pl.ds(start, width), cols]
        for gq in range(A_GROUP):
            hq = kvh * A_GROUP + gq
            u = _dot_nt(qa_ref[:, hq * HEAD_DIM:(hq + 1) * HEAD_DIM], kw)
            u = jnp.where(valid, u - c_ref[hq] * distf, NEG_INF)
            sink = sink_ref[wl, hq] * LOG2E
            m = jnp.maximum(u.max(axis=-1, keepdims=True), sink)
            p = jnp.exp2(u - m)
            den = p.sum(axis=-1, keepdims=True) + jnp.exp2(sink - m)
            o = _dot(p.astype(BF16), vw) / den
            o_ref[:, hq * HEAD_DIM:(hq + 1) * HEAD_DIM] = o.astype(o_ref.dtype)


def _attn_a(proj, sink, wl, batch, seq):
    tq = min(AB_BLOCK_Q, seq)
    nq = seq // tq
    wq = A_Q_HEADS * HEAD_DIM
    wka = A_KV_HEADS * HEAD_DIM
    assert wq % wka == 0
    c = jnp.asarray(_alibi_slopes(N_MIX_HEADS)[:A_Q_HEADS] * np.float32(LOG2E))
    smem = pl.BlockSpec(memory_space=pltpu.SMEM)
    return pl.pallas_call(
        functools.partial(_attn_a_kernel, seq=seq, wl=wl),
        grid=(batch, nq),
        in_specs=[
            smem, smem,
            pl.BlockSpec((tq, wq), lambda b, n: (b * nq + n, 0)),
            pl.BlockSpec((seq, wka), lambda b, n: (b, wq // wka)),
            pl.BlockSpec((seq, wka), lambda b, n: (b, wq // wka + 1)),
        ],
        out_specs=pl.BlockSpec((tq, wq), lambda b, n: (b * nq + n, 0)),
        out_shape=jax.ShapeDtypeStruct((batch * seq, wq), BF16),
        compiler_params=pltpu.CompilerParams(
            dimension_semantics=("parallel", "arbitrary"), vmem_limit_bytes=VMEM_LIMIT_BYTES),
        name="attn_a",
    )(c, sink, proj, proj, proj)


B_TILE_Q = 128
B_FOLD = 4
B_TILES_PER_STEP = 16


def _attn_b_kernel(c_ref, q0_ref, q1_ref, q2_ref, k_ref, v_ref, o_ref, og_ref, lse_ref, bias_ref,
                   fold_ref, ofold_ref, *, seq):
    hb = pl.program_id(1)
    tq = B_TILE_Q
    q_refs = (q0_ref, q1_ref, q2_ref)
    seg = seq // B_FOLD

    for g, (window_len, dil) in enumerate(B_PAIRS):
        radius = window_len // (2 * dil)
        width = min(tq + 2 * radius, seq // dil)
        slope = c_ref[g * B_HEADS + hb] * float(dil)
        for variant, offset in enumerate((0, radius, width - tq)):
            dist = jnp.abs((offset + lax.broadcasted_iota(jnp.int32, (tq, 1), 0))
                           - lax.broadcasted_iota(jnp.int32, (1, width), 1))
            bias_ref[g, variant, :, 0:width] = jnp.where(dist <= radius,
                                                         -slope * dist.astype(F32), NEG_INF)

    def fold_rows(r, first, n):
        return pl.ds((r % B_FOLD) * seg + r // B_FOLD + B_FOLD * first, n, stride=B_FOLD)

    def tile(g, dil, radius, r, blk):
        length = seq // dil
        width = min(tq + 2 * radius, length)
        i0 = blk * tq
        start = jnp.clip(i0 - radius, 0, length - width)
        if dil == 1:
            qrows = pl.ds(pl.multiple_of(i0, tq), tq)
            krows = pl.ds(pl.multiple_of(start, radius), width)
        elif dil == B_FOLD:
            qrows = pl.ds(r + dil * i0, tq, stride=dil)
            krows = pl.ds(r + dil * start, width, stride=dil)
        else:
            qrows = fold_rows(r, i0, tq)
            krows = fold_rows(r, start, width)
        if dil <= B_FOLD:
            q_src, k_src, v_src = q_refs[g], k_ref, v_ref
            og_dst, lse_dst = og_ref.at[g], lse_ref.at[g]
        else:
            q_src, k_src, v_src = fold_ref.at[0], fold_ref.at[1], fold_ref.at[2]
            og_dst, lse_dst = ofold_ref.at[0], ofold_ref.at[1]
        q = q_src[qrows, :].astype(BF16)
        k = k_src[krows, :].astype(BF16)
        v = v_src[krows, :].astype(BF16)
        variant = jnp.where(blk == 0, 0, jnp.where(blk == length // tq - 1, 2, 1))
        u = _dot_nt(q, k) + bias_ref[g, variant, :, 0:width]
        m = u.max(axis=-1, keepdims=True)
        p = jnp.exp2(u - m)
        den = p.sum(axis=-1, keepdims=True)
        og_dst[qrows, :] = _dot(p.astype(BF16), v) / den
        lse_dst[qrows, :] = jnp.broadcast_to(m + jnp.log2(den), (tq, LANES))

    for g, (window_len, dil) in enumerate(B_PAIRS):
        radius = window_len // (2 * dil)
        per_class = (seq // dil) // tq
        n_tiles = dil * per_class
        if dil > B_FOLD:
            for a, src in enumerate((q_refs[g], k_ref, v_ref)):
                for r in range(B_FOLD):
                    fold_ref[a, r * seg:(r + 1) * seg, :] = src[pl.ds(r, seg, stride=B_FOLD), :]

        def body(s, carry, g=g, dil=dil, radius=radius, per_class=per_class):
            for k in range(B_TILES_PER_STEP):
                t = s * B_TILES_PER_STEP + k
                tile(g, dil, radius, t // per_class, t % per_class)
            return carry

        lax.fori_loop(0, n_tiles // B_TILES_PER_STEP, body, 0)
        if dil > B_FOLD:
            for r in range(B_FOLD):
                og_ref[g, pl.ds(r, seg, stride=B_FOLD), :] = ofold_ref[0, r * seg:(r + 1) * seg, :]
                lse_ref[g, pl.ds(r, seg, stride=B_FOLD), :] = ofold_ref[1, r * seg:(r + 1) * seg, :]

    def merge(s, carry):
        rows = pl.ds(pl.multiple_of(s * tq, tq), tq)
        lse = [lse_ref[g, rows, :] for g in range(len(B_PAIRS))]
        top = functools.reduce(jnp.maximum, lse)
        wts = [jnp.exp2(x - top) for x in lse]
        num = sum(w * og_ref[g, rows, :] for g, w in enumerate(wts))
        o_ref[rows, :] = (num / sum(wts)).astype(o_ref.dtype)
        return carry

    lax.fori_loop(0, seq // tq, merge, 0, unroll=2)


def _attn_b(qkv, batch, seq, first=0):
    n_groups = len(B_PAIRS)
    assert all(seq % (d * B_TILE_Q) == 0 and (seq // d // B_TILE_Q * d) % B_TILES_PER_STEP == 0
               and d in (1, B_FOLD, B_FOLD * B_FOLD) for _, d in B_PAIRS)
    c = jnp.asarray(_alibi_slopes(N_MIX_HEADS)[A_Q_HEADS:] * np.float32(LOG2E))
    width_max = max(min(B_TILE_Q + 2 * (w // (2 * d)), seq // d) for w, d in B_PAIRS)

    def slab(head0):
        return pl.BlockSpec((None, seq, HEAD_DIM), lambda b, h: (first + head0 + h, b, 0))

    return pl.pallas_call(
        functools.partial(_attn_b_kernel, seq=seq),
        grid=(batch, B_HEADS),
        in_specs=[pl.BlockSpec(memory_space=pltpu.SMEM)]
        + [slab(g * B_HEADS) for g in range(n_groups)]
        + [slab(n_groups * B_HEADS), slab((n_groups + 1) * B_HEADS)],
        out_specs=pl.BlockSpec((seq, HEAD_DIM), lambda b, h: (b, h)),
        out_shape=jax.ShapeDtypeStruct((batch * seq, B_HEADS * HEAD_DIM), BF16),
        scratch_shapes=[pltpu.VMEM((n_groups, seq, HEAD_DIM), F32),
                        pltpu.VMEM((n_groups, seq, LANES), F32),
                        pltpu.VMEM((n_groups, 3, B_TILE_Q, width_max), F32),
                        pltpu.VMEM((3, seq, HEAD_DIM), F32),
                        pltpu.VMEM((2, seq, HEAD_DIM), F32)],
        compiler_params=pltpu.CompilerParams(
            dimension_semantics=("parallel", "arbitrary"), vmem_limit_bytes=VMEM_LIMIT_BYTES),
        name="attn_b",
    )(c, qkv, qkv, qkv, qkv, qkv)


C_POS_BITS = 6
C_SLOPE_PARTS = 3
C_AUG_POS = 2 * C_SLOPE_PARTS
C_STRIP = 32


def _attn_c_consts():
    c = (_alibi_slopes(C_HEADS).astype(np.float64) * math.log2(math.e)).astype(np.float32)
    parts, rest = [], c.copy()
    for _ in range(C_SLOPE_PARTS):
        p = rest.astype(BF16).astype(np.float32)
        parts.append(p)
        rest = rest - p
    qc = np.zeros((C_HEADS, 1, LANES), np.float32)
    kc = np.zeros((C_HEADS, 1, LANES), np.float32)
    for a, p in enumerate(parts):
        qc[:, 0, C_AUG_POS + 2 * a] = p
        qc[:, 0, C_AUG_POS + 2 * a + 1] = p
        kc[:, 0, 2 * a] = p * (1 << C_POS_BITS)
        kc[:, 0, 2 * a + 1] = p
    return jnp.asarray(c), jnp.asarray(qc), jnp.asarray(kc)


def _attn_c_kernel(c_ref, qc_ref, kc_ref, q_ref, k_ref, v_ref, lam_ref, sg_ref, o_ref,
                   kaug_ref, qv_ref, bias_ref, u_ref, p_ref, m_ref, l_ref, alpha_ref, acc_ref,
                   *, tq, tk, seq, lambda_init):
    h = pl.program_id(1)
    qi = pl.program_id(2)
    nk = seq // tk
    ratio = tk // tq
    kd = qi // ratio
    off = qi - kd * ratio
    lane = lax.broadcasted_iota(jnp.int32, (1, LANES), 1)
    hi_lane = (lane & 1) == 0
    lo_mask = (1 << C_POS_BITS) - 1

    @pl.when(qi == 0)
    def _():
        key_pos_lane = (lane >= C_AUG_POS) & (lane < 2 * C_AUG_POS)

        def fill(kb, carry):
            rows = pl.ds(pl.multiple_of(kb * tk, tk), tk)
            pos = kb * tk + lax.broadcasted_iota(jnp.int32, (tk, 1), 0)
            hi = (pos - (pos & lo_mask)).astype(F32)
            lo = (pos & lo_mask).astype(F32)
            aug = jnp.where(key_pos_lane, -jnp.where(hi_lane, hi, lo), kc_ref[0]).astype(BF16)
            for j in range(2):
                kaug_ref[j, rows, 0:HEAD_DIM] = k_ref[rows, j * HEAD_DIM:(j + 1) * HEAD_DIM]
                kaug_ref[j, rows, HEAD_DIM:2 * HEAD_DIM] = aug
            return carry

        lax.fori_loop(0, nk, fill, 0)
        jj = lax.broadcasted_iota(jnp.int32, (1, tk), 1)
        for o in range(ratio):
            ii = o * tq + lax.broadcasted_iota(jnp.int32, (tq, 1), 0)
            bias_ref[o] = c_ref[h] * jnp.abs(ii - jj).astype(F32)

    qpos = qi * tq + lax.broadcasted_iota(jnp.int32, (tq, 1), 0)
    q_hi = (qpos >> C_POS_BITS).astype(F32)
    q_lo = (qpos & lo_mask).astype(F32)
    q_aug = jnp.where(lane < C_AUG_POS, jnp.where(hi_lane, q_hi, q_lo), qc_ref[0])
    for j in range(2):
        qj = q_ref[:, j * HEAD_DIM:(j + 1) * HEAD_DIM]
        for variant, aug in enumerate((-q_aug, q_aug, jnp.zeros_like(q_aug))):
            qv_ref[variant, j, :, 0:HEAD_DIM] = qj
            qv_ref[variant, j, :, HEAD_DIM:2 * HEAD_DIM] = aug.astype(BF16)

    def key_rows(step):
        kb = kd + step
        kb = jnp.where(kb >= nk, kb - nk, kb)
        return kb, pl.ds(pl.multiple_of(kb * tk, tk), tk)

    def scores(step, slot):
        kb, rows = key_rows(step)
        variant = 2 if step == 0 else jnp.where(kb < kd, 0, 1)
        for j in range(2):
            u_ref[slot, j] = _dot_nt(qv_ref[variant, j], kaug_ref[j, rows, :])

    def accumulate(step, slot):
        _, rows = key_rows(step)
        vblk = v_ref[rows, :]
        for j in range(2):
            for r0 in range(0, tq, C_STRIP):
                rs = slice(r0, r0 + C_STRIP)
                u = u_ref[slot, j, rs, :]
                if step == 0:
                    u = u - bias_ref[off, rs, :]
                    m_new = jnp.broadcast_to(jnp.max(u, axis=1, keepdims=True), (C_STRIP, LANES))
                else:
                    m_prev = m_ref[j, rs, :]
                    m_new = jnp.maximum(m_prev, jnp.max(u, axis=1, keepdims=True))
                    alpha = jnp.exp2(m_prev - m_new)
                    alpha_ref[j, rs, :] = alpha
                p = jnp.exp2(u - jnp.tile(m_new, (1, tk // LANES)))
                psum = p[:, 0:LANES]
                for cb in range(1, tk // LANES):
                    psum = psum + p[:, cb * LANES:(cb + 1) * LANES]
                l_ref[j, rs, :] = psum if step == 0 else alpha * l_ref[j, rs, :] + psum
                m_ref[j, rs, :] = m_new
                p_ref[slot, j, rs, :] = p.astype(BF16)
            pv = _dot(p_ref[slot, j], vblk)
            acc_ref[j] = pv if step == 0 else jnp.tile(alpha_ref[j], (1, 2)) * acc_ref[j] + pv

    scores(0, 0)
    for step in range(nk):
        if step + 1 < nk:
            scores(step + 1, (step + 1) % 2)
        accumulate(step, step % 2)

    lp = lam_ref[...]
    lam = (jnp.exp(jnp.sum(lp[0:1] * lp[1:2], axis=-1, keepdims=True))
           - jnp.exp(jnp.sum(lp[2:3] * lp[3:4], axis=-1, keepdims=True)) + lambda_init)
    den0 = jnp.sum(l_ref[0], axis=1, keepdims=True)
    den1 = jnp.sum(l_ref[1], axis=1, keepdims=True)
    o = acc_ref[0] / den0 - lam * (acc_ref[1] / den1)
    o = (_rms(o) * sg_ref[...]) * (1.0 - lambda_init)
    o_ref[...] = o.astype(o_ref.dtype)


def _attn_c(proj, lam_params, subln_g, wl, lambda_init, batch, seq, tq=512, tk=1024):
    wh = 2 * HEAD_DIM
    d = C_HEADS * wh
    tk = min(tk, seq)
    tq = min(tq, tk)
    nq = seq // tq
    assert seq <= (1 << (2 * C_POS_BITS)) and tk % tq == 0 and tq % C_STRIP == 0 and tk % LANES == 0
    c, qc, kc = _attn_c_consts()
    const_spec = pl.BlockSpec((1, 1, LANES), lambda b, h, qi: (h, 0, 0))
    return pl.pallas_call(
        functools.partial(_attn_c_kernel, tq=tq, tk=tk, seq=seq, lambda_init=lambda_init),
        grid=(batch, C_HEADS, nq),
        in_specs=[
            pl.BlockSpec(memory_space=pltpu.SMEM),
            const_spec, const_spec,
            pl.BlockSpec((tq, wh), lambda b, h, qi: (b * nq + qi, h)),
            pl.BlockSpec((seq, wh), lambda b, h, qi: (b, C_HEADS + h)),
            pl.BlockSpec((seq, wh), lambda b, h, qi: (b, 2 * C_HEADS + h)),
            pl.BlockSpec((None, 4, HEAD_DIM), lambda b, h, qi: (wl, 0, 0)),
            pl.BlockSpec((None, 1, wh), lambda b, h, qi: (wl, 0, 0)),
        ],
        out_specs=pl.BlockSpec((tq, wh), lambda b, h, qi: (b * nq + qi, h)),
        out_shape=jax.ShapeDtypeStruct((batch * seq, d), BF16),
        scratch_shapes=[
            pltpu.VMEM((2, seq, wh), BF16),
            pltpu.VMEM((3, 2, tq, wh), BF16),
            pltpu.VMEM((tk // tq, tq, tk), F32),
            pltpu.VMEM((2, 2, tq, tk), F32),
            pltpu.VMEM((2, 2, tq, tk), BF16),
            pltpu.VMEM((2, tq, LANES), F32),
            pltpu.VMEM((2, tq, LANES), F32),
            pltpu.VMEM((2, tq, LANES), F32),
            pltpu.VMEM((2, tq, wh), F32),
        ],
        compiler_params=pltpu.CompilerParams(
            dimension_semantics=("parallel", "parallel", "arbitrary"),
            vmem_limit_bytes=VMEM_LIMIT_BYTES),
        name="attn_c",
    )(c, qc, kc, proj, proj, proj, lam_params, subln_g.reshape(-1, 1, wh))


def kernel(x, c, ada_w, ada_b, norm_g, ab_w_in, ab_w_out, a_sink, c_w_in, c_w_out, c_lambda,
           c_subln_g, ffn_w_up, ffn_conv_w, ffn_conv_b, ffn_w_down):
    batch, seq, d = x.shape
    depth = ada_w.shape[0]
    dff = ffn_w_down.shape[1]
    dff_pad = -(-dff // FFN_TF) * FFN_TF

    w_in_b, w_out_b = ab_w_in[0:1].astype(BF16), ab_w_out[0:1].astype(BF16)
    w_gate_up = _prep_w_up(ffn_w_up, dff_pad, layers=1)
    w_down = _prep_w_down(ffn_w_down, dff_pad, layers=1)
    conv_w = jnp.pad(ffn_conv_w, ((0, 0), (0, 0), (0, dff_pad - dff)))
    conv_b = jnp.pad(ffn_conv_b, ((0, 0), (0, dff_pad - dff))).reshape(depth, 1, dff_pad)

    q_scale = np.float32(HEAD_DIM ** -0.5 * LOG2E)
    ab_scale = np.ones((ab_w_in.shape[2],), np.float32)
    wq, wkv_a = A_Q_HEADS * HEAD_DIM, 2 * A_KV_HEADS * HEAD_DIM
    ab_scale[:wq] = q_scale
    ab_scale[wq + wkv_a:wq + wkv_a + len(B_PAIRS) * B_HEADS * HEAD_DIM] = q_scale
    ab_col_scale = jnp.asarray(ab_scale)
    c_col_scale = jnp.asarray(np.concatenate([np.full((d,), q_scale, np.float32),
                                              np.ones((2 * d,), np.float32)]))

    c_lanes = jnp.broadcast_to(c[:, :, None], (batch, d, LANES))
    ada_b3 = ada_b.reshape(depth, 1, ada_b.shape[1])
    mod, cond_lanes = _ada(c_lanes, ada_w, ada_b3, 0)
    xs = x.reshape(batch * seq, d)
    for layer in range(depth):
        wl = layer // 2
        last = layer + 1 == depth
        ada_next = None if last else (cond_lanes, ada_w, ada_b3)
        if layer % 2 == 0:
            proj, slabs = _norm_proj(xs, norm_g, mod, layer, w_in_b, 0, ab_col_scale, seq,
                                     tm=AB_PROJ_TM, tn=AB_PROJ_TN, slab_from=0)
            mixed = (_attn_a(proj, a_sink, wl, batch, seq),
                     _attn_b(slabs, batch, seq, first=A_Q_HEADS + 2 * A_KV_HEADS))
        else:
            lambda_init = 0.8 - 0.6 * math.exp(-0.3 * layer)
            proj = _norm_proj(xs, norm_g, mod, layer, w_in_b, 0, c_col_scale, seq, tn=C_PROJ_TN)
            mixed = (_attn_c(proj, c_lambda, c_subln_g, wl, lambda_init, batch, seq),)
        res = _out_proj(mixed, w_out_b, 0, xs, norm_g, mod, layer, seq, ada_next=ada_next)
        xs, mod_next = (res, None) if last else res
        if last:
            raw_next = None
        elif layer % 2 == 0:
            raw_next = (ffn_w_up, ffn_w_down, c_w_in, c_w_out, (layer + 1) // 2)
        else:
            raw_next = (ffn_w_up, ffn_w_down, ab_w_in, ab_w_out, (layer + 1) // 2)
        xs, w_gate_up, w_down, w_in_b, w_out_b = _ffn(
            xs, norm_g, mod, layer, w_gate_up, conv_w, conv_b, w_down, seq, raw_next=raw_next)
        mod = mod_next
    return xs.reshape(batch, seq, d)
```

```python
import functools
import math

import numpy as np
import jax
import jax.numpy as jnp
from jax import lax
from jax.experimental import pallas as pl
from jax.experimental.pallas import tpu as pltpu

F32 = jnp.float32
BF16 = jnp.bfloat16

HEAD_DIM = 128
A_Q_HEADS = 4
A_KV_HEADS = 2
A_GROUP = A_Q_HEADS // A_KV_HEADS
A_RADIUS = 128
B_PAIRS = ((128, 1), (512, 4), (2048, 16))
B_HEADS = 4
N_MIX_HEADS = A_Q_HEADS + len(B_PAIRS) * B_HEADS
C_HEADS = 8
CONV_WIDTH = 3
EPS = 1e-6
NEG_INF = -1e30

SUBLANES_F32 = 8
SUBLANES_BF16 = 16
LANES = 128
VMEM_LIMIT_BYTES = 56 * 1024 * 1024

AB_BLOCK_Q = 256
FFN_HALO = SUBLANES_BF16
FFN_TF = 512
AB_PROJ_TM = 512
AB_PROJ_TN = 1792
C_PROJ_TN = 1536
LOG2E = math.log2(math.e)


def _alibi_slopes(n):
    return (2.0 ** (-8.0 * np.arange(1, n + 1, dtype=np.float32) / n)).astype(np.float32)


def _rms(x):
    return x * lax.rsqrt(jnp.mean(x * x, axis=-1, keepdims=True) + EPS)


def _dot(a, b):
    return jnp.dot(a, b, preferred_element_type=F32)


def _dot_nt(a, b):
    return lax.dot_general(a, b, (((1,), (1,)), ((), ())), preferred_element_type=F32)


def _ada_rows(cond_ref, w_ref, b_ref, o_ref):
    batch, d, _ = cond_ref.shape
    tn = w_ref.shape[1]
    acc = [jnp.zeros((SUBLANES_F32, tn), F32) for _ in range(batch)]
    for r0 in range(0, d, SUBLANES_F32):
        w8 = w_ref[r0:r0 + SUBLANES_F32, :]
        for b in range(batch):
            acc[b] = acc[b] + w8 * jnp.tile(cond_ref[b, r0:r0 + SUBLANES_F32, :], (1, tn // LANES))
    for b in range(batch):
        o_ref[b:b + 1, :] = jnp.sum(acc[b], axis=0, keepdims=True) + b_ref[...]
    o_ref[batch:, :] = jnp.zeros((o_ref.shape[0] - batch, tn), F32)


def _ada_kernel(c_ref, w_ref, b_ref, o_ref, cond_ref):
    @pl.when(pl.program_id(0) == 0)
    def _():
        c = c_ref[...]
        cond_ref[...] = c * jax.nn.sigmoid(c)

    _ada_rows(cond_ref, w_ref, b_ref, o_ref)


def _ada(c_lanes, ada_w, ada_b3, layer, tn=1024):
    _, d, n = ada_w.shape
    assert c_lanes.shape[0] < SUBLANES_F32
    c_spec = pl.BlockSpec(c_lanes.shape, lambda j: (0, 0, 0))
    return pl.pallas_call(
        _ada_kernel,
        grid=(n // tn,),
        in_specs=[
            c_spec,
            pl.BlockSpec((None, d, tn), lambda j: (layer, 0, j)),
            pl.BlockSpec((None, 1, tn), lambda j: (layer, 0, j)),
        ],
        out_specs=[pl.BlockSpec((SUBLANES_F32, tn), lambda j: (0, j)), c_spec],
        out_shape=[jax.ShapeDtypeStruct((SUBLANES_F32, n), F32),
                   jax.ShapeDtypeStruct(c_lanes.shape, F32)],
        compiler_params=pltpu.CompilerParams(
            dimension_semantics=("arbitrary",), vmem_limit_bytes=VMEM_LIMIT_BYTES),
        name="ada",
    )(c_lanes, ada_w, ada_b3)


MOD_SH1, MOD_SC1, MOD_G1, MOD_SH2, MOD_SC2, MOD_G2 = range(6)
NORM_PRE_MIX, NORM_POST_MIX, NORM_PRE_FFN, NORM_POST_FFN = range(4)


def _mod_spec(chunk, d):
    return pl.BlockSpec((SUBLANES_F32, d), lambda *_: (0, chunk))


def _norm_spec(layer, d):
    return pl.BlockSpec((None, 4, d), lambda *_: (layer, 0, 0))


def _batch_row(ref, b):
    return ref[pl.ds(b, 1), :]


NORM_STRIP = 32
NORM_UNROLL = 8


def _strip_rows(s):
    return pl.ds(pl.multiple_of(s * NORM_STRIP, NORM_STRIP), NORM_STRIP)


def _row_rsqrt_ms(src_ref, rs_ref):
    n_rows, d = src_ref.shape

    def body(s, carry):
        rows = _strip_rows(s)
        x = src_ref[rows, :]
        ms = jnp.sum(x * x, axis=1, keepdims=True) * (1.0 / d)
        rs_ref[rows, :] = jnp.broadcast_to(lax.rsqrt(ms + EPS), (NORM_STRIP, LANES))
        return carry

    lax.fori_loop(0, n_rows // NORM_STRIP, body, 0, unroll=NORM_UNROLL)


def _scaled_rows(src_ref, rs_ref, rows):
    return src_ref[rows, :] * jnp.tile(rs_ref[rows, :], (1, src_ref.shape[1] // LANES))


def _cast_pad_kernel(w_ref, o_ref, *, axis):
    n = w_ref.shape[axis]
    if axis == 0:
        o_ref[:n, :] = w_ref[...].astype(o_ref.dtype)
        o_ref[n:, :] = jnp.zeros((o_ref.shape[0] - n, o_ref.shape[1]), o_ref.dtype)
    else:
        o_ref[:, :n] = w_ref[...].astype(o_ref.dtype)
        o_ref[:, n:] = jnp.zeros((o_ref.shape[0], o_ref.shape[1] - n), o_ref.dtype)


def _prep_w_up(w_up, dff_pad, layers, tr=256):
    _, d, two_dff = w_up.shape
    dff = two_dff // 2
    return pl.pallas_call(
        functools.partial(_cast_pad_kernel, axis=1),
        grid=(layers, 2, d // tr),
        in_specs=[pl.BlockSpec((None, tr, dff), lambda l, p, r: (l, r, p))],
        out_specs=pl.BlockSpec((None, None, tr, dff_pad), lambda l, p, r: (l, p, r, 0)),
        out_shape=jax.ShapeDtypeStruct((layers, 2, d, dff_pad), BF16),
        compiler_params=pltpu.CompilerParams(
            dimension_semantics=("arbitrary",) * 3, vmem_limit_bytes=VMEM_LIMIT_BYTES),
        name="prep_w_up",
    )(w_up)


def _prep_w_down(w_down, dff_pad, layers, tc=256):
    _, dff, d = w_down.shape
    return pl.pallas_call(
        functools.partial(_cast_pad_kernel, axis=0),
        grid=(layers, d // tc),
        in_specs=[pl.BlockSpec((None, dff, tc), lambda l, c: (l, 0, c))],
        out_specs=pl.BlockSpec((None, dff_pad, tc), lambda l, c: (l, 0, c)),
        out_shape=jax.ShapeDtypeStruct((layers, dff_pad, d), BF16),
        compiler_params=pltpu.CompilerParams(
            dimension_semantics=("arbitrary",) * 2, vmem_limit_bytes=VMEM_LIMIT_BYTES),
        name="prep_w_down",
    )(w_down)


def _norm_proj_kernel(x_ref, g_ref, sc_ref, sh_ref, w_ref, cs_ref, o_ref, *rest, per_seq, slab_from):
    if slab_from is None:
        h_ref, rs_ref = rest
    else:
        slab_ref, h_ref, rs_ref = rest
    j = pl.program_id(1)

    @pl.when(j == 0)
    def _():
        b = pl.program_id(0) // per_seq
        g_mod = g_ref[NORM_PRE_MIX:NORM_PRE_MIX + 1, :] * (1.0 + _batch_row(sc_ref, b))
        shift = _batch_row(sh_ref, b)
        _row_rsqrt_ms(x_ref, rs_ref)

        def body(s, carry):
            rows = _strip_rows(s)
            h_ref[rows, :] = (_scaled_rows(x_ref, rs_ref, rows) * g_mod + shift).astype(BF16)
            return carry

        lax.fori_loop(0, x_ref.shape[0] // NORM_STRIP, body, 0, unroll=NORM_UNROLL)

    res = _dot(h_ref[...], w_ref[...]) * cs_ref[...]
    o_ref[...] = res.astype(o_ref.dtype)
    if slab_from is not None:
        @pl.when(j >= slab_from)
        def _():
            for hh in range(slab_ref.shape[0]):
                slab_ref[hh] = res[:, hh * LANES:(hh + 1) * LANES]


def _norm_proj(x, norm_g, mod, layer, w, wl, col_scale, seq, tm=1024, tn=512, slab_from=None):
    m, d = x.shape
    n = w.shape[2]
    tm = min(tm, seq)
    nj = n // tn
    out_specs = pl.BlockSpec((tm, tn), lambda i, j: (i, j))
    out_shape = jax.ShapeDtypeStruct((m, n), BF16)
    if slab_from is not None:
        per_block = tn // LANES
        out_specs = [out_specs, pl.BlockSpec((per_block, tm, LANES),
                                             lambda i, j: (jnp.maximum(j - slab_from, 0), i, 0))]
        out_shape = [out_shape, jax.ShapeDtypeStruct(((nj - slab_from) * per_block, m, LANES), F32)]
    return pl.pallas_call(
        functools.partial(_norm_proj_kernel, per_seq=seq // tm, slab_from=slab_from),
        grid=(m // tm, nj),
        in_specs=[
            pl.BlockSpec((tm, d), lambda i, j: (i, 0)),
            _norm_spec(layer, d),
            _mod_spec(MOD_SC1, d),
            _mod_spec(MOD_SH1, d),
            pl.BlockSpec((None, d, tn), lambda i, j: (wl, 0, j)),
            pl.BlockSpec((1, tn), lambda i, j: (0, j)),
        ],
        out_specs=out_specs,
        out_shape=out_shape,
        scratch_shapes=[pltpu.VMEM((tm, d), BF16), pltpu.VMEM((tm, LANES), F32)],
        compiler_params=pltpu.CompilerParams(
            dimension_semantics=("parallel", "arbitrary"), vmem_limit_bytes=VMEM_LIMIT_BYTES),
        name="norm_proj",
    )(x, norm_g, mod, mod, w, col_scale.reshape(1, n))


def _out_proj_kernel(*refs, per_seq, widths, ada_next):
    a_refs = refs[:len(widths)]
    if ada_next:
        w_ref, x_ref, g_ref, gate_ref, c_ref, aw_ref, ab_ref, o_ref, mod_ref = refs[len(widths):]
        _ada_rows(c_ref, aw_ref, ab_ref, mod_ref)
    else:
        w_ref, x_ref, g_ref, gate_ref, o_ref = refs[len(widths):]
    b = pl.program_id(0) // per_seq
    y, k0 = None, 0
    for a_ref, k in zip(a_refs, widths):
        part = _dot(a_ref[...], w_ref[k0:k0 + k, :])
        y = part if y is None else y + part
        k0 += k
    g = g_ref[NORM_POST_MIX:NORM_POST_MIX + 1, :]
    o_ref[...] = x_ref[...] + _batch_row(gate_ref, b) * (_rms(y) * g)


def _out_proj(mixed, w, wl, x, norm_g, mod, layer, seq, ada_next=None, tm=512):
    m = x.shape[0]
    widths = tuple(a.shape[1] for a in mixed)
    k, d = w.shape[1], w.shape[2]
    assert sum(widths) == k
    tm = min(tm, seq)
    steps = m // tm
    out_specs = pl.BlockSpec((tm, d), lambda i: (i, 0))
    out_shape = jax.ShapeDtypeStruct((m, d), F32)
    ada_specs, ada_args = [], []
    if ada_next is not None:
        c_lanes, ada_w, ada_b3 = ada_next
        n = ada_w.shape[2]
        tn = n // steps
        assert tn * steps == n and tn % LANES == 0
        ada_specs = [
            pl.BlockSpec(c_lanes.shape, lambda i: (0, 0, 0), pipeline_mode=pl.Buffered(1)),
            pl.BlockSpec((None, d, tn), lambda i: (layer + 1, 0, i)),
            pl.BlockSpec((None, 1, tn), lambda i: (layer + 1, 0, i)),
        ]
        ada_args = [c_lanes, ada_w, ada_b3]
        out_specs = [out_specs, pl.BlockSpec((SUBLANES_F32, tn), lambda i: (0, i))]
        out_shape = [out_shape, jax.ShapeDtypeStruct((SUBLANES_F32, n), F32)]
    return pl.pallas_call(
        functools.partial(_out_proj_kernel, per_seq=seq // tm, widths=widths,
                          ada_next=ada_next is not None),
        grid=(steps,),
        in_specs=[pl.BlockSpec((tm, kk), lambda i: (i, 0)) for kk in widths] + [
            pl.BlockSpec((None, k, d), lambda i: (wl, 0, 0)),
            pl.BlockSpec((tm, d), lambda i: (i, 0)),
            _norm_spec(layer, d),
            _mod_spec(MOD_G1, d),
        ] + ada_specs,
        out_specs=out_specs,
        out_shape=out_shape,
        compiler_params=pltpu.CompilerParams(
            dimension_semantics=("arbitrary",), vmem_limit_bytes=VMEM_LIMIT_BYTES),
        name="out_proj",
    )(*mixed, w, x, norm_g, mod, *ada_args)


def _ffn_kernel(x_ref, xp_ref, xn_ref, g_ref, sc_ref, sh_ref, wg_ref, wu_ref, cw_ref, cb_ref,
                wd_ref, gate_ref, *rest, tm, per_seq, prep_next, down_chunks):
    i = pl.program_id(0)
    f = pl.program_id(1)
    b = i // per_seq
    halo = FFN_HALO
    d = x_ref.shape[1]
    if prep_next:
        (ng_ref, nu_ref, nd_ref, mi_ref, mo_ref, o_ref, ogu_ref, owd_ref, omi_ref, omo_ref,
         h_ref, rs_ref) = rest
    else:
        o_ref, h_ref, rs_ref = rest

    @pl.when(f == 0)
    def _():
        g_mod = g_ref[NORM_PRE_FFN:NORM_PRE_FFN + 1, :] * (1.0 + _batch_row(sc_ref, b))
        shift = _batch_row(sh_ref, b)

        keep_prev = jnp.where(i % per_seq == 0, 0.0, 1.0)
        keep_next = jnp.where(i % per_seq == per_seq - 1, 0.0, 1.0)
        zeros = jnp.zeros((halo - SUBLANES_F32, d), F32)
        hp = (_rms(xp_ref[...]) * g_mod + shift) * keep_prev
        hn = (_rms(xn_ref[...]) * g_mod + shift) * keep_next
        h_ref[0:halo, :] = jnp.concatenate([zeros, hp], axis=0).astype(BF16)
        h_ref[halo + tm:, :] = jnp.concatenate([hn, zeros], axis=0).astype(BF16)

        _row_rsqrt_ms(x_ref, rs_ref)

        def body(s, carry):
            rows = _strip_rows(s)
            dst = pl.ds(pl.multiple_of(halo + s * NORM_STRIP, SUBLANES_BF16), NORM_STRIP)
            h_ref[dst, :] = (_scaled_rows(x_ref, rs_ref, rows) * g_mod + shift).astype(BF16)
            return carry

        lax.fori_loop(0, tm // NORM_STRIP, body, 0, unroll=NORM_UNROLL)
        o_ref[...] = jnp.zeros_like(o_ref)

    rows = tm + 2 * halo
    gate = _dot(h_ref[...], wg_ref[...])
    up = _dot(h_ref[halo:halo + tm, :], wu_ref[...])
    g_prev = pltpu.roll(gate, 1, 0)[halo:halo + tm]
    g_next = pltpu.roll(gate, rows - 1, 0)[halo:halo + tm]
    g_mid = gate[halo:halo + tm]
    cw = cw_ref[...]
    z = g_prev * cw[0:1] + g_mid * cw[1:2] + g_next * cw[2:3] + cb_ref[...]
    act = jax.nn.gelu(z, approximate=True) * up
    o_ref[...] += _dot(act.astype(BF16), wd_ref[...])

    if prep_next:
        dff = ng_ref.shape[1]
        for half, src in enumerate((ng_ref, nu_ref)):
            ogu_ref[half, :, 0:dff] = src[...].astype(BF16)
            ogu_ref[half, :, dff:] = jnp.zeros((src.shape[0], ogu_ref.shape[2] - dff), BF16)
        is_data = i * pl.num_programs(1) + f < down_chunks
        owd_ref[...] = jnp.where(is_data, nd_ref[...], 0.0).astype(BF16)
        omi_ref[...] = mi_ref[...].astype(BF16)
        omo_ref[...] = mo_ref[...].astype(BF16)

    @pl.when(f == pl.num_programs(1) - 1)
    def _():
        g_gate = g_ref[NORM_POST_FFN:NORM_POST_FFN + 1, :] * _batch_row(gate_ref, b)
        _row_rsqrt_ms(o_ref, rs_ref)

        def body(s, carry):
            rows = _strip_rows(s)
            o_ref[rows, :] = x_ref[rows, :] + _scaled_rows(o_ref, rs_ref, rows) * g_gate
            return carry

        lax.fori_loop(0, tm // NORM_STRIP, body, 0, unroll=NORM_UNROLL)


FFN_PREP_UP_ROWS = 32
FFN_PREP_DOWN_ROWS = 128


def _ffn(x, norm_g, mod, layer, w_gate_up, conv_w, conv_b, w_down, seq, raw_next=None,
         tm=1024, tf=FFN_TF):
    m, d = x.shape
    dff_pad = w_down.shape[1]
    tm = min(tm, seq)
    per_seq = seq // tm
    nblk8 = m // SUBLANES_F32
    r8 = tm // SUBLANES_F32
    nf = dff_pad // tf
    prep_specs, prep_args, prep_out_specs, prep_out_shapes, down_chunks = [], [], [], [], None
    if raw_next is not None:
        raw_up, raw_down, mix_in, mix_out, mix_layer = raw_next
        dff = raw_down.shape[1]
        steps = (m // tm) * nf
        up_rows = FFN_PREP_UP_ROWS
        while d // up_rows > steps:
            up_rows *= 2
        up_chunks = d // up_rows
        down_chunks = dff // FFN_PREP_DOWN_ROWS
        pad_chunks = (dff_pad - dff) // FFN_PREP_DOWN_ROWS
        assert d % up_rows == 0 and dff % FFN_PREP_DOWN_ROWS == 0
        assert pad_chunks * FFN_PREP_DOWN_ROWS == dff_pad - dff and pad_chunks <= 1
        assert down_chunks + pad_chunks <= steps
        nxt = layer + 1

        def up_chunk(i, f):
            return jnp.minimum(i * nf + f, up_chunks - 1)

        prep_specs = [
            pl.BlockSpec((None, up_rows, dff), lambda i, f: (nxt, up_chunk(i, f), 0)),
            pl.BlockSpec((None, up_rows, dff), lambda i, f: (nxt, up_chunk(i, f), 1)),
            pl.BlockSpec((None, FFN_PREP_DOWN_ROWS, d),
                         lambda i, f: (nxt, jnp.minimum(i * nf + f, down_chunks - 1), 0)),
        ]
        prep_args = [raw_up, raw_up, raw_down, mix_in, mix_out]
        for w in (mix_in, mix_out):
            rows = SUBLANES_BF16
            while w.shape[1] // rows > steps:
                rows *= 2
            assert w.shape[1] % rows == 0
            chunks = w.shape[1] // rows
            prep_specs.append(pl.BlockSpec(
                (None, rows, w.shape[2]),
                lambda i, f, chunks=chunks: (mix_layer, jnp.minimum(i * nf + f, chunks - 1), 0)))
        prep_out_specs = [
            pl.BlockSpec((2, up_rows, dff_pad), lambda i, f: (0, up_chunk(i, f), 0)),
            pl.BlockSpec((FFN_PREP_DOWN_ROWS, d),
                         lambda i, f: (jnp.minimum(i * nf + f, down_chunks + pad_chunks - 1), 0)),
        ]
        prep_out_shapes = [jax.ShapeDtypeStruct((2, d, dff_pad), BF16),
                           jax.ShapeDtypeStruct((dff_pad, d), BF16)]
        for spec, w in zip(prep_specs[3:], (mix_in, mix_out)):
            rows, chunks = spec.block_shape[1], w.shape[1] // spec.block_shape[1]
            prep_out_specs.append(pl.BlockSpec(
                (rows, w.shape[2]),
                lambda i, f, chunks=chunks: (jnp.minimum(i * nf + f, chunks - 1), 0)))
            prep_out_shapes.append(jax.ShapeDtypeStruct(w.shape[1:], BF16))
    x_out_spec = pl.BlockSpec((tm, d), lambda i, f: (i, 0), pipeline_mode=pl.Buffered(1))
    x_out_shape = jax.ShapeDtypeStruct((m, d), F32)
    outs = pl.pallas_call(
        functools.partial(_ffn_kernel, tm=tm, per_seq=per_seq, prep_next=raw_next is not None,
                          down_chunks=down_chunks),
        grid=(m // tm, nf),
        in_specs=[
            pl.BlockSpec((tm, d), lambda i, f: (i, 0)),
            pl.BlockSpec((SUBLANES_F32, d), lambda i, f: (jnp.maximum(i * r8 - 1, 0), 0)),
            pl.BlockSpec((SUBLANES_F32, d), lambda i, f: (jnp.minimum((i + 1) * r8, nblk8 - 1), 0)),
            _norm_spec(layer, d),
            _mod_spec(MOD_SC2, d),
            _mod_spec(MOD_SH2, d),
            pl.BlockSpec((None, None, d, tf), lambda i, f: (0, 0, 0, f)),
            pl.BlockSpec((None, None, d, tf), lambda i, f: (0, 1, 0, f)),
            pl.BlockSpec((None, CONV_WIDTH, tf), lambda i, f: (layer, 0, f)),
            pl.BlockSpec((None, 1, tf), lambda i, f: (layer, 0, f)),
            pl.BlockSpec((None, tf, d), lambda i, f: (0, f, 0)),
            _mod_spec(MOD_G2, d),
        ] + prep_specs,
        out_specs=[x_out_spec] + prep_out_specs,
        out_shape=[x_out_shape] + prep_out_shapes,
        scratch_shapes=[pltpu.VMEM((tm + 2 * FFN_HALO, d), BF16), pltpu.VMEM((tm, LANES), F32)],
        compiler_params=pltpu.CompilerParams(
            dimension_semantics=("arbitrary", "arbitrary"), vmem_limit_bytes=VMEM_LIMIT_BYTES),
        name="ffn",
    )(x, x, x, norm_g, mod, mod, w_gate_up, w_gate_up, conv_w, conv_b, w_down, mod, *prep_args)
    if raw_next is None:
        return outs[0], None, None, None, None
    return (outs[0],) + tuple(o[None] for o in outs[1:])


def _attn_a_kernel(c_ref, sink_ref, qa_ref, ka_ref, va_ref, o_ref, *, seq, wl):
    tq = AB_BLOCK_Q
    t0 = pl.program_id(1) * tq
    width = min(tq + 2 * A_RADIUS, seq)
    start = pl.multiple_of(jnp.clip(t0 - A_RADIUS, 0, seq - width), SUBLANES_BF16)
    qpos = t0 + lax.broadcasted_iota(jnp.int32, (tq, 1), 0)
    kpos = start + lax.broadcasted_iota(jnp.int32, (1, width), 1)
    dist = jnp.abs(qpos - kpos)
    valid = dist <= A_RADIUS
    distf = dist.astype(F32)
    for kvh in range(A_KV_HEADS):
        cols = slice(kvh * HEAD_DIM, (kvh + 1) * HEAD_DIM)
        kw = ka_ref[pl.ds(start, width), cols]
        vw = va_ref[pl.ds(start, width), cols]
        for gq in range(A_GROUP):
            hq = kvh * A_GROUP + gq
            u = _dot_nt(qa_ref[:, hq * HEAD_DIM:(hq + 1) * HEAD_DIM], kw)
            u = jnp.where(valid, u - c_ref[hq] * distf, NEG_INF)
            sink = sink_ref[wl, hq] * LOG2E
            m = jnp.maximum(u.max(axis=-1, keepdims=True), sink)
            p = jnp.exp2(u - m)
            den = p.sum(axis=-1, keepdims=True) + jnp.exp2(sink - m)
            o = _dot(p.astype(BF16), vw) / den
            o_ref[:, hq * HEAD_DIM:(hq + 1) * HEAD_DIM] = o.astype(o_ref.dtype)


def _attn_a(proj, sink, wl, batch, seq):
    tq = min(AB_BLOCK_Q, seq)
    nq = seq // tq
    wq = A_Q_HEADS * HEAD_DIM
    wka = A_KV_HEADS * HEAD_DIM
    assert wq % wka == 0
    c = jnp.asarray(_alibi_slopes(N_MIX_HEADS)[:A_Q_HEADS] * np.float32(LOG2E))
    smem = pl.BlockSpec(memory_space=pltpu.SMEM)
    return pl.pallas_call(
        functools.partial(_attn_a_kernel, seq=seq, wl=wl),
        grid=(batch, nq),
        in_specs=[
            smem, smem,
            pl.BlockSpec((tq, wq), lambda b, n: (b * nq + n, 0)),
            pl.BlockSpec((seq, wka), lambda b, n: (b, wq // wka)),
            pl.BlockSpec((seq, wka), lambda b, n: (b, wq // wka + 1)),
        ],
        out_specs=pl.BlockSpec((tq, wq), lambda b, n: (b * nq + n, 0)),
        out_shape=jax.ShapeDtypeStruct((batch * seq, wq), BF16),
        compiler_params=pltpu.CompilerParams(
            dimension_semantics=("parallel", "arbitrary"), vmem_limit_bytes=VMEM_LIMIT_BYTES),
        name="attn_a",
    )(c, sink, proj, proj, proj)


B_TILE_Q = 128
B_FOLD = 4
B_TILES_PER_STEP = 32


def _attn_b_kernel(c_ref, q0_ref, q1_ref, q2_ref, k_ref, v_ref, o_ref, og_ref, lse_ref, bias_ref,
                   fold_ref, ofold_ref, *, seq):
    hb = pl.program_id(1)
    tq = B_TILE_Q
    q_refs = (q0_ref, q1_ref, q2_ref)
    seg = seq // B_FOLD

    for g, (window_len, dil) in enumerate(B_PAIRS):
        radius = window_len // (2 * dil)
        width = min(tq + 2 * radius, seq // dil)
        slope = c_ref[g * B_HEADS + hb] * float(dil)
        for variant, offset in enumerate((0, radius, width - tq)):
            dist = jnp.abs((offset + lax.broadcasted_iota(jnp.int32, (tq, 1), 0))
                           - lax.broadcasted_iota(jnp.int32, (1, width), 1))
            bias_ref[g, variant, :, 0:width] = jnp.where(dist <= radius,
                                                         -slope * dist.astype(F32), NEG_INF)

    def fold_rows(r, first, n):
        return pl.ds((r % B_FOLD) * seg + r // B_FOLD + B_FOLD * first, n, stride=B_FOLD)

    def tile(g, dil, radius, r, blk):
        length = seq // dil
        width = min(tq + 2 * radius, length)
        i0 = blk * tq
        start = jnp.clip(i0 - radius, 0, length - width)
        if dil == 1:
            qrows = pl.ds(pl.multiple_of(i0, tq), tq)
            krows = pl.ds(pl.multiple_of(start, radius), width)
        elif dil == B_FOLD:
            qrows = pl.ds(r + dil * i0, tq, stride=dil)
            krows = pl.ds(r + dil * start, width, stride=dil)
        else:
            qrows = fold_rows(r, i0, tq)
            krows = fold_rows(r, start, width)
        if dil <= B_FOLD:
            q_src, k_src, v_src = q_refs[g], k_ref, v_ref
            og_dst, lse_dst = og_ref.at[g], lse_ref.at[g]
        else:
            q_src, k_src, v_src = fold_ref.at[0], fold_ref.at[1], fold_ref.at[2]
            og_dst, lse_dst = ofold_ref.at[0], ofold_ref.at[1]
        q = q_src[qrows, :].astype(BF16)
        k = k_src[krows, :].astype(BF16)
        v = v_src[krows, :].astype(BF16)
        variant = jnp.where(blk == 0, 0, jnp.where(blk == length // tq - 1, 2, 1))
        u = _dot_nt(q, k) + bias_ref[g, variant, :, 0:width]
        m = u.max(axis=-1, keepdims=True)
        p = jnp.exp2(u - m)
        den = p.sum(axis=-1, keepdims=True)
        og_dst[qrows, :] = _dot(p.astype(BF16), v) / den
        lse_dst[qrows, :] = jnp.broadcast_to(m + jnp.log2(den), (tq, LANES))

    for g, (window_len, dil) in enumerate(B_PAIRS):
        radius = window_len // (2 * dil)
        per_class = (seq // dil) // tq
        n_tiles = dil * per_class
        if dil > B_FOLD:
            for a, src in enumerate((q_refs[g], k_ref, v_ref)):
                for r in range(B_FOLD):
                    fold_ref[a, r * seg:(r + 1) * seg, :] = src[pl.ds(r, seg, stride=B_FOLD), :]

        def body(s, carry, g=g, dil=dil, radius=radius, per_class=per_class):
            for k in range(B_TILES_PER_STEP):
                t = s * B_TILES_PER_STEP + k
                tile(g, dil, radius, t // per_class, t % per_class)
            return carry

        lax.fori_loop(0, n_tiles // B_TILES_PER_STEP, body, 0)
        if dil > B_FOLD:
            for r in range(B_FOLD):
                og_ref[g, pl.ds(r, seg, stride=B_FOLD), :] = ofold_ref[0, r * seg:(r + 1) * seg, :]
                lse_ref[g, pl.ds(r, seg, stride=B_FOLD), :] = ofold_ref[1, r * seg:(r + 1) * seg, :]

    def merge(s, carry):
        rows = pl.ds(pl.multiple_of(s * tq, tq), tq)
        lse = [lse_ref[g, rows, :] for g in range(len(B_PAIRS))]
        top = functools.reduce(jnp.maximum, lse)
        wts = [jnp.exp2(x - top) for x in lse]
        num = sum(w * og_ref[g, rows, :] for g, w in enumerate(wts))
        o_ref[rows, :] = (num / sum(wts)).astype(o_ref.dtype)
        return carry

    lax.fori_loop(0, seq // tq, merge, 0, unroll=2)


def _attn_b(qkv, batch, seq, first=0):
    n_groups = len(B_PAIRS)
    assert all(seq % (d * B_TILE_Q) == 0 and (seq // d // B_TILE_Q * d) % B_TILES_PER_STEP == 0
               and d in (1, B_FOLD, B_FOLD * B_FOLD) for _, d in B_PAIRS)
    c = jnp.asarray(_alibi_slopes(N_MIX_HEADS)[A_Q_HEADS:] * np.float32(LOG2E))
    width_max = max(min(B_TILE_Q + 2 * (w // (2 * d)), seq // d) for w, d in B_PAIRS)

    def slab(head0):
        return pl.BlockSpec((None, seq, HEAD_DIM), lambda b, h: (first + head0 + h, b, 0))

    return pl.pallas_call(
        functools.partial(_attn_b_kernel, seq=seq),
        grid=(batch, B_HEADS),
        in_specs=[pl.BlockSpec(memory_space=pltpu.SMEM)]
        + [slab(g * B_HEADS) for g in range(n_groups)]
        + [slab(n_groups * B_HEADS), slab((n_groups + 1) * B_HEADS)],
        out_specs=pl.BlockSpec((seq, HEAD_DIM), lambda b, h: (b, h)),
        out_shape=jax.ShapeDtypeStruct((batch * seq, B_HEADS * HEAD_DIM), BF16),
        scratch_shapes=[pltpu.VMEM((n_groups, seq, HEAD_DIM), F32),
                        pltpu.VMEM((n_groups, seq, LANES), F32),
                        pltpu.VMEM((n_groups, 3, B_TILE_Q, width_max), F32),
                        pltpu.VMEM((3, seq, HEAD_DIM), F32),
                        pltpu.VMEM((2, seq, HEAD_DIM), F32)],
        compiler_params=pltpu.CompilerParams(
            dimension_semantics=("parallel", "arbitrary"), vmem_limit_bytes=VMEM_LIMIT_BYTES),
        name="attn_b",
    )(c, qkv, qkv, qkv, qkv, qkv)


C_POS_BITS = 6
C_SLOPE_PARTS = 3
C_AUG_POS = 2 * C_SLOPE_PARTS
C_STRIP = 32


def _attn_c_consts():
    c = (_alibi_slopes(C_HEADS).astype(np.float64) * math.log2(math.e)).astype(np.float32)
    parts, rest = [], c.copy()
    for _ in range(C_SLOPE_PARTS):
        p = rest.astype(BF16).astype(np.float32)
        parts.append(p)
        rest = rest - p
    qc = np.zeros((C_HEADS, 1, LANES), np.float32)
    kc = np.zeros((C_HEADS, 1, LANES), np.float32)
    for a, p in enumerate(parts):
        qc[:, 0, C_AUG_POS + 2 * a] = p
        qc[:, 0, C_AUG_POS + 2 * a + 1] = p
        kc[:, 0, 2 * a] = p * (1 << C_POS_BITS)
        kc[:, 0, 2 * a + 1] = p
    return jnp.asarray(c), jnp.asarray(qc), jnp.asarray(kc)


def _attn_c_kernel(c_ref, qc_ref, kc_ref, q_ref, k_ref, v_ref, lam_ref, sg_ref, o_ref,
                   kaug_ref, qv_ref, bias_ref, u_ref, p_ref, m_ref, l_ref, alpha_ref, acc_ref,
                   *, tq, tk, seq, lambda_init):
    h = pl.program_id(1)
    qi = pl.program_id(2)
    nk = seq // tk
    ratio = tk // tq
    kd = qi // ratio
    off = qi - kd * ratio
    lane = lax.broadcasted_iota(jnp.int32, (1, LANES), 1)
    hi_lane = (lane & 1) == 0
    lo_mask = (1 << C_POS_BITS) - 1

    @pl.when(qi == 0)
    def _():
        key_pos_lane = (lane >= C_AUG_POS) & (lane < 2 * C_AUG_POS)

        def fill(kb, carry):
            rows = pl.ds(pl.multiple_of(kb * tk, tk), tk)
            pos = kb * tk + lax.broadcasted_iota(jnp.int32, (tk, 1), 0)
            hi = (pos - (pos & lo_mask)).astype(F32)
            lo = (pos & lo_mask).astype(F32)
            aug = jnp.where(key_pos_lane, -jnp.where(hi_lane, hi, lo), kc_ref[0]).astype(BF16)
            for j in range(2):
                kaug_ref[j, rows, 0:HEAD_DIM] = k_ref[rows, j * HEAD_DIM:(j + 1) * HEAD_DIM]
                kaug_ref[j, rows, HEAD_DIM:2 * HEAD_DIM] = aug
            return carry

        lax.fori_loop(0, nk, fill, 0)
        jj = lax.broadcasted_iota(jnp.int32, (1, tk), 1)
        for o in range(ratio):
            ii = o * tq + lax.broadcasted_iota(jnp.int32, (tq, 1), 0)
            bias_ref[o] = c_ref[h] * jnp.abs(ii - jj).astype(F32)

    qpos = qi * tq + lax.broadcasted_iota(jnp.int32, (tq, 1), 0)
    q_hi = (qpos >> C_POS_BITS).astype(F32)
    q_lo = (qpos & lo_mask).astype(F32)
    q_aug = jnp.where(lane < C_AUG_POS, jnp.where(hi_lane, q_hi, q_lo), qc_ref[0])
    for j in range(2):
        qj = q_ref[:, j * HEAD_DIM:(j + 1) * HEAD_DIM]
        for variant, aug in enumerate((-q_aug, q_aug, jnp.zeros_like(q_aug))):
            qv_ref[variant, j, :, 0:HEAD_DIM] = qj
            qv_ref[variant, j, :, HEAD_DIM:2 * HEAD_DIM] = aug.astype(BF16)

    def key_rows(step):
        kb = kd + step
        kb = jnp.where(kb >= nk, kb - nk, kb)
        return kb, pl.ds(pl.multiple_of(kb * tk, tk), tk)

    def scores(step, slot):
        kb, rows = key_rows(step)
        variant = 2 if step == 0 else jnp.where(kb < kd, 0, 1)
        for j in range(2):
            u_ref[slot, j] = _dot_nt(qv_ref[variant, j], kaug_ref[j, rows, :])

    def accumulate(step, slot):
        _, rows = key_rows(step)
        vblk = v_ref[rows, :]
        for j in range(2):
            for r0 in range(0, tq, C_STRIP):
                rs = slice(r0, r0 + C_STRIP)
                u = u_ref[slot, j, rs, :]
                if step == 0:
                    u = u - bias_ref[off, rs, :]
                    m_new = jnp.broadcast_to(jnp.max(u, axis=1, keepdims=True), (C_STRIP, LANES))
                else:
                    m_prev = m_ref[j, rs, :]
                    m_new = jnp.maximum(m_prev, jnp.max(u, axis=1, keepdims=True))
                    alpha = jnp.exp2(m_prev - m_new)
                    alpha_ref[j, rs, :] = alpha
                p = jnp.exp2(u - jnp.tile(m_new, (1, tk // LANES)))
                psum = p[:, 0:LANES]
                for cb in range(1, tk // LANES):
                    psum = psum + p[:, cb * LANES:(cb + 1) * LANES]
                l_ref[j, rs, :] = psum if step == 0 else alpha * l_ref[j, rs, :] + psum
                m_ref[j, rs, :] = m_new
                p_ref[slot, j, rs, :] = p.astype(BF16)
            pv = _dot(p_ref[slot, j], vblk)
            acc_ref[j] = pv if step == 0 else jnp.tile(alpha_ref[j], (1, 2)) * acc_ref[j] + pv

    scores(0, 0)
    for step in range(nk):
        if step + 1 < nk:
            scores(step + 1, (step + 1) % 2)
        accumulate(step, step % 2)

    lp = lam_ref[...]
    lam = (jnp.exp(jnp.sum(lp[0:1] * lp[1:2], axis=-1, keepdims=True))
           - jnp.exp(jnp.sum(lp[2:3] * lp[3:4], axis=-1, keepdims=True)) + lambda_init)
    den0 = jnp.sum(l_ref[0], axis=1, keepdims=True)
    den1 = jnp.sum(l_ref[1], axis=1, keepdims=True)
    o = acc_ref[0] / den0 - lam * (acc_ref[1] / den1)
    o = (_rms(o) * sg_ref[...]) * (1.0 - lambda_init)
    o_ref[...] = o.astype(o_ref.dtype)


def _attn_c(proj, lam_params, subln_g, wl, lambda_init, batch, seq, tq=512, tk=1024):
    wh = 2 * HEAD_DIM
    d = C_HEADS * wh
    tk = min(tk, seq)
    tq = min(tq, tk)
    nq = seq // tq
    assert seq <= (1 << (2 * C_POS_BITS)) and tk % tq == 0 and tq % C_STRIP == 0 and tk % LANES == 0
    c, qc, kc = _attn_c_consts()
    const_spec = pl.BlockSpec((1, 1, LANES), lambda b, h, qi: (h, 0, 0))
    return pl.pallas_call(
        functools.partial(_attn_c_kernel, tq=tq, tk=tk, seq=seq, lambda_init=lambda_init),
        grid=(batch, C_HEADS, nq),
        in_specs=[
            pl.BlockSpec(memory_space=pltpu.SMEM),
            const_spec, const_spec,
            pl.BlockSpec((tq, wh), lambda b, h, qi: (b * nq + qi, h)),
            pl.BlockSpec((seq, wh), lambda b, h, qi: (b, C_HEADS + h)),
            pl.BlockSpec((seq, wh), lambda b, h, qi: (b, 2 * C_HEADS + h)),
            pl.BlockSpec((None, 4, HEAD_DIM), lambda b, h, qi: (wl, 0, 0)),
            pl.BlockSpec((None, 1, wh), lambda b, h, qi: (wl, 0, 0)),
        ],
        out_specs=pl.BlockSpec((tq, wh), lambda b, h, qi: (b * nq + qi, h)),
        out_shape=jax.ShapeDtypeStruct((batch * seq, d), BF16),
        scratch_shapes=[
            pltpu.VMEM((2, seq, wh), BF16),
            pltpu.VMEM((3, 2, tq, wh), BF16),
            pltpu.VMEM((tk // tq, tq, tk), F32),
            pltpu.VMEM((2, 2, tq, tk), F32),
            pltpu.VMEM((2, 2, tq, tk), BF16),
            pltpu.VMEM((2, tq, LANES), F32),
            pltpu.VMEM((2, tq, LANES), F32),
            pltpu.VMEM((2, tq, LANES), F32),
            pltpu.VMEM((2, tq, wh), F32),
        ],
        compiler_params=pltpu.CompilerParams(
            dimension_semantics=("parallel", "parallel", "arbitrary"),
            vmem_limit_bytes=VMEM_LIMIT_BYTES),
        name="attn_c",
    )(c, qc, kc, proj, proj, proj, lam_params, subln_g.reshape(-1, 1, wh))


def kernel(x, c, ada_w, ada_b, norm_g, ab_w_in, ab_w_out, a_sink, c_w_in, c_w_out, c_lambda,
           c_subln_g, ffn_w_up, ffn_conv_w, ffn_conv_b, ffn_w_down):
    batch, seq, d = x.shape
    depth = ada_w.shape[0]
    dff = ffn_w_down.shape[1]
    dff_pad = -(-dff // FFN_TF) * FFN_TF

    w_in_b, w_out_b = ab_w_in[0:1].astype(BF16), ab_w_out[0:1].astype(BF16)
    w_gate_up = _prep_w_up(ffn_w_up, dff_pad, layers=1)
    w_down = _prep_w_down(ffn_w_down, dff_pad, layers=1)
    conv_w = jnp.pad(ffn_conv_w, ((0, 0), (0, 0), (0, dff_pad - dff)))
    conv_b = jnp.pad(ffn_conv_b, ((0, 0), (0, dff_pad - dff))).reshape(depth, 1, dff_pad)

    q_scale = np.float32(HEAD_DIM ** -0.5 * LOG2E)
    ab_scale = np.ones((ab_w_in.shape[2],), np.float32)
    wq, wkv_a = A_Q_HEADS * HEAD_DIM, 2 * A_KV_HEADS * HEAD_DIM
    ab_scale[:wq] = q_scale
    ab_scale[wq + wkv_a:wq + wkv_a + len(B_PAIRS) * B_HEADS * HEAD_DIM] = q_scale
    ab_col_scale = jnp.asarray(ab_scale)
    c_col_scale = jnp.asarray(np.concatenate([np.full((d,), q_scale, np.float32),
                                              np.ones((2 * d,), np.float32)]))

    c_lanes = jnp.broadcast_to(c[:, :, None], (batch, d, LANES))
    ada_b3 = ada_b.reshape(depth, 1, ada_b.shape[1])
    mod, cond_lanes = _ada(c_lanes, ada_w, ada_b3, 0)
    xs = x.reshape(batch * seq, d)
    for layer in range(depth):
        wl = layer // 2
        last = layer + 1 == depth
        ada_next = None if last else (cond_lanes, ada_w, ada_b3)
        if layer % 2 == 0:
            proj, slabs = _norm_proj(xs, norm_g, mod, layer, w_in_b, 0, ab_col_scale, seq,
                                     tm=AB_PROJ_TM, tn=AB_PROJ_TN, slab_from=0)
            mixed = (_attn_a(proj, a_sink, wl, batch, seq),
                     _attn_b(slabs, batch, seq, first=A_Q_HEADS + 2 * A_KV_HEADS))
        else:
            lambda_init = 0.8 - 0.6 * math.exp(-0.3 * layer)
            proj = _norm_proj(xs, norm_g, mod, layer, w_in_b, 0, c_col_scale, seq, tn=C_PROJ_TN)
            mixed = (_attn_c(proj, c_lambda, c_subln_g, wl, lambda_init, batch, seq),)
        res = _out_proj(mixed, w_out_b, 0, xs, norm_g, mod, layer, seq, ada_next=ada_next)
        xs, mod_next = (res, None) if last else res
        if last:
            raw_next = None
        elif layer % 2 == 0:
            raw_next = (ffn_w_up, ffn_w_down, c_w_in, c_w_out, (layer + 1) // 2)
        else:
            raw_next = (ffn_w_up, ffn_w_down, ab_w_in, ab_w_out, (layer + 1) // 2)
        xs, w_gate_up, w_down, w_in_b, w_out_b = _ffn(
            xs, norm_g, mod, layer, w_gate_up, conv_w, conv_b, w_down, seq, raw_next=raw_next)
        mod = mod_next
    return xs.reshape(batch, seq, d)
```
